```python
import math
import jax, jax.numpy as jnp
from jax import lax
import numpy as np

D_MODEL = 4096
BATCH = 4
SEQ = 2048
DEPTH = 2
DEC_BATCH = 8
DEC_SEQ = 4
PAST_LEN = 16384
PAGE_SIZE = 128

GDN_HEADS = 32
GDN_DK = 128
GDN_DV = 128
GDN_CONV = 4
GDN_CHUNK = 64
GDN_QK_W = GDN_HEADS * GDN_DK
GDN_V_W = GDN_HEADS * GDN_DV
GDN_CONV_W = 2 * GDN_QK_W + GDN_V_W
GDN_IN_W = GDN_CONV_W + GDN_V_W + 2 * GDN_HEADS
ATT_HEADS = 32
ATT_KV_HEADS = 8
ATT_GROUP = ATT_HEADS // ATT_KV_HEADS
ATT_HEAD_DIM = 128
IDX_HEADS = 64
IDX_DIM = 128
IDX_TOPK = 256
ATT_BLOCK = 128
ATT_Q_W = ATT_HEADS * ATT_HEAD_DIM
ATT_KV_W = ATT_KV_HEADS * ATT_HEAD_DIM
IDX_Q_W = IDX_HEADS * IDX_DIM
Q_END = ATT_Q_W
K_END = Q_END + ATT_KV_W
V_END = K_END + ATT_KV_W
QI_END = V_END + IDX_Q_W
KI_END = QI_END + IDX_DIM
ATT_IN_W = KI_END + IDX_HEADS
D_FF = 11008
N_EXPERTS = 8
MOE_TOPK = 2
D_FF_EXPERT = 14336
NORM_EPS = 1e-6
N_GDN_LAYERS = (DEPTH + 1) // 2
N_ATT_LAYERS = DEPTH // 2

kernel_name = 'hybrid_gdn_dsa_moe_adaln_step'


def rmsnorm(x, g):
    xf = x.astype(jnp.float32)
    y = xf * lax.rsqrt(jnp.mean(xf * xf, axis=-1, keepdims=True) + NORM_EPS)
    return (y * g.astype(jnp.float32)).astype(x.dtype)


def l2norm(x):
    return x * lax.rsqrt(jnp.sum(x * x, axis=-1, keepdims=True) + NORM_EPS)


def adaln(c, w, b):
    return jnp.split(jax.nn.silu(c) @ w + b, 6, axis=-1)


def modulate(x, g, shift, scale):
    return rmsnorm(x, g) * (1 + scale[:, None, :]) + shift[:, None, :]


def causal_conv(u, buf, w):
    T = u.shape[1]
    up = jnp.concatenate([buf, u], axis=1)
    y = up[:, :T] * w[0]
    for j in range(1, GDN_CONV):
        y = y + up[:, j:j + T] * w[j]
    return jax.nn.silu(y), up[:, -(GDN_CONV - 1):]


def gated_delta_rule(q, k, v, g, beta, s0, chunk):
    B, T, H, dk = q.shape
    dv = v.shape[-1]
    n = T // chunk

    def to_chunks(a):
        return jnp.moveaxis(a.reshape((B, n, chunk, H) + a.shape[3:]), 3, 1)

    qc, kc, vc = to_chunks(q), to_chunks(k), to_chunks(v)
    gc = jnp.cumsum(to_chunks(g), axis=-1)
    bc = to_chunks(beta)
    tri = jnp.tril(jnp.ones((chunk, chunk), bool))
    stri = jnp.tril(jnp.ones((chunk, chunk), bool), -1)
    diff = gc[..., :, None] - gc[..., None, :]
    decay = jnp.where(tri, jnp.exp(jnp.where(tri, diff, 0.0)), 0.0)
    kb = kc * bc[..., None]
    a_mat = jnp.where(stri, jnp.einsum('bhnid,bhnjd->bhnij', kb, kc) * decay, 0.0)
    lhs = a_mat + jnp.eye(chunk, dtype=a_mat.dtype)
    rhs = jnp.concatenate([vc * bc[..., None], kb * jnp.exp(gc)[..., None]], axis=-1)
    sol = lax.linalg.triangular_solve(lhs, rhs, left_side=True, lower=True, unit_diagonal=True)
    u, w = sol[..., :dv], sol[..., dv:]
    attn = jnp.where(tri, jnp.einsum('bhnid,bhnjd->bhnij', qc, kc) * decay, 0.0)

    def step(s, xs):
        q_i, k_i, u_i, w_i, g_i, a_i = xs
        v_new = u_i - jnp.einsum('bhck,bhkv->bhcv', w_i, s)
        o = (jnp.einsum('bhck,bhkv->bhcv', q_i * jnp.exp(g_i)[..., None], s)
             + jnp.einsum('bhij,bhjv->bhiv', a_i, v_new))
        g_last = g_i[..., -1]
        s = (s * jnp.exp(g_last)[..., None, None]
             + jnp.einsum('bhck,bhcv->bhkv', k_i * jnp.exp(g_last[..., None] - g_i)[..., None], v_new))
        return s, o

    xs = tuple(jnp.moveaxis(a, 2, 0) for a in (qc, kc, u, w, gc, attn))
    s_fin, o = lax.scan(step, s0, xs)
    o = jnp.transpose(o, (1, 0, 3, 2, 4)).reshape(B, T, H, dv)
    return o, s_fin


def gdn_mixer(h, conv_buf, s0, chunk, w_in, conv_w, a_log, dt_bias, o_norm, w_out):
    B, T, _ = h.shape
    f32 = jnp.float32
    proj = h @ w_in
    o0, o1, o2 = GDN_CONV_W, GDN_CONV_W + GDN_V_W, GDN_CONV_W + GDN_V_W + GDN_HEADS
    u, z, b_logit, a_logit = proj[..., :o0], proj[..., o0:o1], proj[..., o1:o2], proj[..., o2:]
    u, new_buf = causal_conv(u, conv_buf, conv_w)
    u = u.astype(f32)
    q = l2norm(u[..., :GDN_QK_W].reshape(B, T, GDN_HEADS, GDN_DK)) * (GDN_DK ** -0.5)
    k = l2norm(u[..., GDN_QK_W:2 * GDN_QK_W].reshape(B, T, GDN_HEADS, GDN_DK))
    v = u[..., 2 * GDN_QK_W:].reshape(B, T, GDN_HEADS, GDN_DV)
    beta = jax.nn.sigmoid(b_logit.astype(f32))
    g = -jnp.exp(a_log.astype(f32)) * jax.nn.softplus(a_logit.astype(f32) + dt_bias.astype(f32))
    o, s_new = gated_delta_rule(q, k, v, g, beta, s0.astype(f32), chunk)
    o = rmsnorm(o, o_norm) * jax.nn.silu(z.astype(f32).reshape(B, T, GDN_HEADS, GDN_DV))
    out = o.reshape(B, T, GDN_V_W).astype(h.dtype) @ w_out
    return out, new_buf, s_new.astype(s0.dtype)


def split_att(proj):
    B, T, _ = proj.shape
    q = proj[..., :Q_END].reshape(B, T, ATT_KV_HEADS, ATT_GROUP, ATT_HEAD_DIM)
    k = proj[..., Q_END:K_END].reshape(B, T, ATT_KV_HEADS, ATT_HEAD_DIM)
    v = proj[..., K_END:V_END].reshape(B, T, ATT_KV_HEADS, ATT_HEAD_DIM)
    qi = proj[..., V_END:QI_END].reshape(B, T, IDX_HEADS, IDX_DIM)
    ki = proj[..., QI_END:KI_END]
    wi = proj[..., KI_END:]
    return q, k, v, qi, ki, wi


def indexer_scores(qi, wi, ki):
    dots = jnp.einsum('bqhd,bsd->bqhs', qi, ki, preferred_element_type=jnp.float32)
    return jnp.einsum('bqhs,bqh->bqs', jax.nn.relu(dots), wi.astype(jnp.float32)) * (IDX_DIM ** -0.5 * IDX_HEADS ** -0.5)


def select_keys(scores, q_pos, n_keys, topk):
    allowed = jnp.arange(n_keys)[None, :] <= q_pos[:, None]
    _, idx = lax.top_k(jnp.where(allowed[None], scores, -jnp.inf), topk)
    valid = idx <= q_pos[None, :, None]
    return idx, valid


def sparse_attend(q, k_sel, v_sel, valid):
    s = jnp.einsum('bqgrd,bqkgd->bqgrk', q, k_sel, preferred_element_type=jnp.float32) * (ATT_HEAD_DIM ** -0.5)
    p = jax.nn.softmax(jnp.where(valid[:, :, None, None, :], s, -jnp.inf), axis=-1)
    return jnp.einsum('bqgrk,bqkgd->bqgrd', p.astype(v_sel.dtype), v_sel)


def dsa_prompt(h, w_in, w_out):
    B, T, _ = h.shape
    q, k, v, qi, ki, wi = split_att(h @ w_in)
    topk = min(IDX_TOPK, T // 4)
    bidx = jnp.arange(B)[:, None, None]

    def block(i):
        sl = lambda a: lax.dynamic_slice_in_dim(a, i * ATT_BLOCK, ATT_BLOCK, axis=1)
        q_pos = i * ATT_BLOCK + jnp.arange(ATT_BLOCK)
        idx, valid = select_keys(indexer_scores(sl(qi), sl(wi), ki), q_pos, T, topk)
        return sparse_attend(sl(q), k[bidx, idx], v[bidx, idx], valid)

    o = lax.map(block, jnp.arange(T // ATT_BLOCK))
    o = jnp.moveaxis(o, 0, 1).reshape(B, T, ATT_Q_W)
    return o @ w_out, k, v, ki


def dsa_sample(h, k_pool, v_pool, ki_pool, page_table, w_in, w_out):
    B, T, _ = h.shape
    q, k, v, qi, ki, wi = split_att(h @ w_in)
    n_past = page_table.shape[1] * PAGE_SIZE
    L = n_past + T
    topk = min(IDX_TOPK, L // 4)
    ki_past = ki_pool[page_table].reshape(B, n_past, IDX_DIM)
    ki_all = jnp.concatenate([ki_past, ki], axis=1)
    q_pos = n_past + jnp.arange(T)
    idx, valid = select_keys(indexer_scores(qi, wi, ki_all), q_pos, L, topk)
    is_past = (idx < n_past)[..., None, None]
    pidx = jnp.minimum(idx, n_past - 1)
    phys = jnp.take_along_axis(page_table, (pidx // PAGE_SIZE).reshape(B, -1), axis=1).reshape(idx.shape)
    off = pidx % PAGE_SIZE
    nidx = jnp.clip(idx - n_past, 0, T - 1)
    bidx = jnp.arange(B)[:, None, None]
    k_sel = jnp.where(is_past, k_pool[phys, off], k[bidx, nidx])
    v_sel = jnp.where(is_past, v_pool[phys, off], v[bidx, nidx])
    o = sparse_attend(q, k_sel, v_sel, valid).reshape(B, T, ATT_Q_W)
    return o @ w_out, k, v, ki


def swiglu(h, w1, w3, w2):
    return (jax.nn.silu(h @ w1) * (h @ w3)) @ w2


def moe(h, router_w, w1, w3, w2):
    B, T, D = h.shape
    t = h.reshape(B * T, D)
    logits = (t @ router_w).astype(jnp.float32)
    top_v, top_i = lax.top_k(logits, MOE_TOPK)
    gates = jax.nn.softmax(top_v, axis=-1)
    out = jnp.zeros_like(t)
    for e in range(N_EXPERTS):
        w_e = jnp.sum(jnp.where(top_i == e, gates, 0.0), axis=-1)
        out = out + swiglu(t, w1[e], w3[e], w2[e]) * w_e[:, None].astype(t.dtype)
    return out.reshape(B, T, D)


def setup_inputs(seed: int = 0) -> dict:
    key = jax.random.key(seed)
    ks = jax.random.split(key, 32)

    def nrm(i, shape, scale):
        return jax.random.normal(ks[i], shape, jnp.float32) * scale

    n_pages = PAST_LEN // PAGE_SIZE
    n_pool = (DEC_BATCH * n_pages * 5) // 4
    page_table = jax.random.permutation(ks[0], n_pool)[:DEC_BATCH * n_pages].reshape(DEC_BATCH, n_pages).astype(jnp.int32)
    dt = jnp.exp(jax.random.uniform(ks[17], (N_GDN_LAYERS, GDN_HEADS), jnp.float32, math.log(1e-3), math.log(1e-1)))
    dt_bias = dt + jnp.log(-jnp.expm1(-dt))
    a_log = jnp.log(jax.random.uniform(ks[16], (N_GDN_LAYERS, GDN_HEADS), jnp.float32, 1.0, 16.0))
    D = D_MODEL
    return {
        'x_prompt': nrm(1, (BATCH, SEQ, D), 1.0),
        'x_sample': nrm(2, (DEC_BATCH, DEC_SEQ, D), 1.0),
        'state_gdn': nrm(3, (N_GDN_LAYERS, DEC_BATCH, GDN_HEADS, GDN_DK, GDN_DV), 0.1),
        'state_conv': nrm(4, (N_GDN_LAYERS, DEC_BATCH, GDN_CONV - 1, GDN_CONV_W), 1.0),
        'cache_k': nrm(5, (N_ATT_LAYERS, n_pool, PAGE_SIZE, ATT_KV_HEADS, ATT_HEAD_DIM), 1.0),
        'cache_v': nrm(6, (N_ATT_LAYERS, n_pool, PAGE_SIZE, ATT_KV_HEADS, ATT_HEAD_DIM), 1.0),
        'cache_kidx': nrm(7, (N_ATT_LAYERS, n_pool, PAGE_SIZE, IDX_DIM), 1.0),
        'page_table': page_table,
        'c_prompt': nrm(8, (BATCH, D), 1.0),
        'c_sample': nrm(9, (DEC_BATCH, D), 1.0),
        'ada_w': nrm(10, (DEPTH, D, 6 * D), 0.5 * D ** -0.5),
        'ada_b': nrm(11, (DEPTH, 6 * D), 0.02),
        'norm1_g': 1.0 + nrm(12, (DEPTH, D), 0.01),
        'norm2_g': 1.0 + nrm(13, (DEPTH, D), 0.01),
        'gdn_w_in': nrm(14, (N_GDN_LAYERS, D, GDN_IN_W), D ** -0.5),
        'gdn_conv_w': nrm(15, (N_GDN_LAYERS, GDN_CONV, GDN_CONV_W), GDN_CONV ** -0.5),
        'gdn_a_log': a_log,
        'gdn_dt_bias': dt_bias,
        'gdn_o_norm': 1.0 + nrm(18, (N_GDN_LAYERS, GDN_DV), 0.01),
        'gdn_w_out': nrm(19, (N_GDN_LAYERS, GDN_V_W, D), GDN_V_W ** -0.5),
        'att_w_in': nrm(20, (N_ATT_LAYERS, D, ATT_IN_W), D ** -0.5),
        'att_w_out': nrm(21, (N_ATT_LAYERS, ATT_Q_W, D), ATT_Q_W ** -0.5),
        'ffn_w1': nrm(22, (N_GDN_LAYERS, D, D_FF), D ** -0.5),
        'ffn_w3': nrm(23, (N_GDN_LAYERS, D, D_FF), D ** -0.5),
        'ffn_w2': nrm(24, (N_GDN_LAYERS, D_FF, D), D_FF ** -0.5),
        'moe_router_w': nrm(25, (N_ATT_LAYERS, D, N_EXPERTS), D ** -0.5),
        'moe_w1': nrm(26, (N_ATT_LAYERS, N_EXPERTS, D, D_FF_EXPERT), D ** -0.5),
        'moe_w3': nrm(27, (N_ATT_LAYERS, N_EXPERTS, D, D_FF_EXPERT), D ** -0.5),
        'moe_w2': nrm(28, (N_ATT_LAYERS, N_EXPERTS, D_FF_EXPERT, D), D_FF_EXPERT ** -0.5),
        'final_norm_g': 1.0 + nrm(29, (D,), 0.01),
    }


def reference(x_prompt, x_sample, state_gdn, state_conv, cache_k, cache_v, cache_kidx, page_table,
              c_prompt, c_sample, ada_w, ada_b, norm1_g, norm2_g,
              gdn_w_in, gdn_conv_w, gdn_a_log, gdn_dt_bias, gdn_o_norm, gdn_w_out,
              att_w_in, att_w_out, ffn_w1, ffn_w3, ffn_w2,
              moe_router_w, moe_w1, moe_w3, moe_w2, final_norm_g):
    xp, xs = x_prompt, x_sample
    gdn_s_p, gdn_c_p, gdn_s_s, gdn_c_s = [], [], [], []
    k_p, v_p, ki_p, k_s, v_s, ki_s = [], [], [], [], [], []
    for layer in range(DEPTH):
        j = layer // 2
        mp = adaln(c_prompt, ada_w[layer], ada_b[layer])
        ms = adaln(c_sample, ada_w[layer], ada_b[layer])
        hp = modulate(xp, norm1_g[layer], mp[0], mp[1])
        hs = modulate(xs, norm1_g[layer], ms[0], ms[1])
        if layer % 2 == 0:
            bp = xp.shape[0]
            buf0 = jnp.zeros((bp, GDN_CONV - 1, GDN_CONV_W), hp.dtype)
            s0 = jnp.zeros((bp, GDN_HEADS, GDN_DK, GDN_DV), state_gdn.dtype)
            ap, cbp, sp = gdn_mixer(hp, buf0, s0, GDN_CHUNK, gdn_w_in[j], gdn_conv_w[j], gdn_a_log[j],
                                    gdn_dt_bias[j], gdn_o_norm[j], gdn_w_out[j])
            a_s, cbs, ss = gdn_mixer(hs, state_conv[j], state_gdn[j], hs.shape[1], gdn_w_in[j], gdn_conv_w[j],
                                     gdn_a_log[j], gdn_dt_bias[j], gdn_o_norm[j], gdn_w_out[j])
            gdn_s_p.append(sp)
            gdn_c_p.append(cbp)
            gdn_s_s.append(ss)
            gdn_c_s.append(cbs)
        else:
            ap, kp1, vp1, kip1 = dsa_prompt(hp, att_w_in[j], att_w_out[j])
            a_s, ks1, vs1, kis1 = dsa_sample(hs, cache_k[j], cache_v[j], cache_kidx[j], page_table,
                                             att_w_in[j], att_w_out[j])
            k_p.append(kp1)
            v_p.append(vp1)
            ki_p.append(kip1)
            k_s.append(ks1)
            v_s.append(vs1)
            ki_s.append(kis1)
        xp = xp + mp[2][:, None, :] * ap
        xs = xs + ms[2][:, None, :] * a_s
        hp = modulate(xp, norm2_g[layer], mp[3], mp[4])
        hs = modulate(xs, norm2_g[layer], ms[3], ms[4])
        if layer % 2 == 0:
            fp = swiglu(hp, ffn_w1[j], ffn_w3[j], ffn_w2[j])
            fs = swiglu(hs, ffn_w1[j], ffn_w3[j], ffn_w2[j])
        else:
            rw, w1, w3, w2 = moe_router_w[j], moe_w1[j], moe_w3[j], moe_w2[j]
            fp = moe(hp, rw, w1, w3, w2)
            fs = moe(hs, rw, w1, w3, w2)
        xp = xp + mp[5][:, None, :] * fp
        xs = xs + ms[5][:, None, :] * fs
    y_prompt = rmsnorm(xp, final_norm_g)
    y_sample = rmsnorm(xs, final_norm_g)
    return (y_prompt, y_sample,
            jnp.stack(gdn_s_p), jnp.stack(gdn_c_p), jnp.stack(gdn_s_s), jnp.stack(gdn_c_s),
            jnp.stack(k_p), jnp.stack(v_p), jnp.stack(ki_p),
            jnp.stack(k_s), jnp.stack(v_s), jnp.stack(ki_s))
```

```python
import functools

import jax
import jax.numpy as jnp
from jax import lax
from jax.experimental import pallas as pl
from jax.experimental.pallas import tpu as pltpu

F32 = jnp.float32
BF16 = jnp.bfloat16
I32 = jnp.int32
HIGHEST = lax.Precision.HIGHEST

LANES = 128
VMEM_CAP_V7X = 60 * 1024 * 1024
NORM_EPS = 1e-6
GDN_CHUNK = 64
IDX_TOPK = 256
ATT_BLOCK = 128
MOE_TOPK = 2
INT32_MIN = -(2 ** 31)


def _params(semantics, vmem_bytes):
    limit = int(min(max(vmem_bytes, 32 * 1024 * 1024), VMEM_CAP_V7X))
    return pltpu.CompilerParams(dimension_semantics=semantics, vmem_limit_bytes=limit)


def _silu(x):
    return x / (1.0 + jnp.exp(-x))


def _dot_nt(a, b, precision=None):
    return lax.dot_general(a, b, (((1,), (1,)), ((), ())), preferred_element_type=F32, precision=precision)


def _dot_tn(a, b, precision=None):
    return lax.dot_general(a, b, (((0,), (0,)), ((), ())), preferred_element_type=F32, precision=precision)


def _modulate_kernel(*refs, modulated, n_experts):
    it = iter(refs)
    x_ref, g_ref = next(it), next(it)
    sh_ref = sc_ref = rw_ref = None
    if modulated:
        sh_ref, sc_ref = next(it), next(it)
    if n_experts:
        rw_ref = next(it)
    o_ref = next(it)
    x = x_ref[...]
    y = x * lax.rsqrt(jnp.mean(x * x, axis=-1, keepdims=True) + NORM_EPS) * g_ref[...]
    if modulated:
        y = y * (1.0 + sc_ref[0]) + sh_ref[0]
    o_ref[...] = y.astype(o_ref.dtype)
    if n_experts:
        ridx_ref, rgate_ref = next(it), next(it)
        logits = jnp.dot(y, rw_ref[...], preferred_element_type=F32, precision=HIGHEST)
        lane = lax.broadcasted_iota(I32, logits.shape, 1)
        l1 = jnp.where(lane < n_experts, logits, -jnp.inf)
        m1 = jnp.max(l1, axis=1, keepdims=True)
        i1 = jnp.min(jnp.where(l1 == m1, lane, LANES), axis=1, keepdims=True)
        l2 = jnp.where(lane == i1, -jnp.inf, l1)
        m2 = jnp.max(l2, axis=1, keepdims=True)
        i2 = jnp.min(jnp.where(l2 == m2, lane, LANES), axis=1, keepdims=True)
        e1 = jnp.exp(m2 - m1)
        den = 1.0 + e1
        ridx_ref[...] = jnp.where(lane == 0, i1, jnp.where(lane == 1, i2, 0))
        rgate_ref[...] = jnp.where(lane == 0, 1.0 / den, jnp.where(lane == 1, e1 / den, 0.0))


def _modulate(x, g, shift, scale, *, rows_per_group, tm, out_dtype, router_w=None, n_experts=0):
    M, D = x.shape
    modulated = shift is not None
    tiles_per_group = rows_per_group // tm
    in_specs = [pl.BlockSpec((tm, D), lambda i: (i, 0)), pl.BlockSpec((1, D), lambda i: (0, 0))]
    args = [x, g.reshape(1, D)]
    if modulated:
        R = shift.shape[1]
        spec = pl.BlockSpec((1, R, D), lambda i: (i // tiles_per_group, 0, 0))
        in_specs += [spec, spec]
        args += [shift, scale]
    out_shape = [jax.ShapeDtypeStruct((M, D), out_dtype)]
    out_specs = [pl.BlockSpec((tm, D), lambda i: (i, 0))]
    if n_experts:
        in_specs.append(pl.BlockSpec((D, LANES), lambda i: (0, 0)))
        args.append(router_w)
        out_shape += [jax.ShapeDtypeStruct((M, LANES), I32), jax.ShapeDtypeStruct((M, LANES), F32)]
        out_specs += [pl.BlockSpec((tm, LANES), lambda i: (i, 0))] * 2
    vmem = 2 * tm * D * (4 + jnp.dtype(out_dtype).itemsize) + 6 * tm * D * 4 + 16 * D * 4
    res = pl.pallas_call(
        functools.partial(_modulate_kernel, modulated=modulated, n_experts=n_experts),
        grid=(M // tm,), in_specs=in_specs, out_specs=out_specs, out_shape=out_shape,
        compiler_params=_params(("arbitrary",), vmem), name="modulate")(*args)
    return res if n_experts else res[0]


def _mm_kernel(*refs, a_silu, head_major, has_bias, has_resid):
    it = iter(refs)
    a_ref, w_ref = next(it), next(it)
    b_ref = next(it) if has_bias else None
    r_ref, gt_ref = (next(it), next(it)) if has_resid else (None, None)
    o_ref, wbf_ref = next(it), next(it)

    @pl.when(pl.program_id(1) == 0)
    def _():
        wbf_ref[...] = w_ref[0].astype(BF16)

    a = a_ref[...]
    if a_silu:
        a = _silu(a).astype(BF16)
    acc = jnp.dot(a, wbf_ref[...], preferred_element_type=F32)
    if has_bias:
        acc = acc + b_ref[0]
    if has_resid:
        acc = r_ref[...] + gt_ref[0] * acc
    if head_major:
        for j in range(o_ref.shape[0]):
            o_ref[j] = acc[:, j * LANES:(j + 1) * LANES].astype(o_ref.dtype)
    else:
        o_ref[...] = acc.astype(o_ref.dtype)


def _matmul(a, w, wl, *, col0=0, ncols=None, tm, tn, out_dtype=F32, head_major=False, a_silu=False,
            bias=None, resid=None, gate=None, rows_per_group=None):
    M, K = a.shape
    ncols = w.shape[2] - col0 if ncols is None else ncols
    assert col0 % tn == 0 and M % tm == 0
    n_tiles, off = pl.cdiv(ncols, tn), col0 // tn
    in_specs = [pl.BlockSpec((tm, K), lambda n, m: (m, 0)),
                pl.BlockSpec((1, K, tn), lambda n, m: (wl, 0, n + off))]
    args = [a, w]
    if bias is not None:
        in_specs.append(pl.BlockSpec((1, 1, tn), lambda n, m: (wl, 0, n + off)))
        args.append(bias)
    if resid is not None:
        tiles_per_group = rows_per_group // tm
        in_specs += [pl.BlockSpec((tm, tn), lambda n, m: (m, n)),
                     pl.BlockSpec((1, gate.shape[1], tn), lambda n, m: (m // tiles_per_group, 0, n))]
        args += [resid, gate]
    if head_major:
        assert ncols % tn == 0
        out_shape = jax.ShapeDtypeStruct((ncols // LANES, M, LANES), out_dtype)
        out_spec = pl.BlockSpec((tn // LANES, tm, LANES), lambda n, m: (n, m, 0))
    else:
        out_shape = jax.ShapeDtypeStruct((M, ncols), out_dtype)
        out_spec = pl.BlockSpec((tm, tn), lambda n, m: (m, n))
    osz = jnp.dtype(out_dtype).itemsize
    vmem = (2 * tm * K * a.dtype.itemsize + 2 * K * tn * 4 + K * tn * 2 + 2 * tm * tn * osz
            + 4 * tm * tn * 4 + (2 * tm * tn * 4 if resid is not None else 0))
    return pl.pallas_call(
        functools.partial(_mm_kernel, a_silu=a_silu, head_major=head_major, has_bias=bias is not None,
                          has_resid=resid is not None),
        grid=(n_tiles, M // tm), in_specs=in_specs, out_specs=out_spec, out_shape=out_shape,
        scratch_shapes=[pltpu.VMEM((K, tn), BF16)],
        compiler_params=_params(("arbitrary", "arbitrary"), vmem), name="matmul")(*args)


def _swiglu_kernel(*refs, grouped):
    if grouped:
        fresh_ref, live_ref = refs[1], refs[2]
        refs = refs[4:]
    a_ref, w1_ref, w3_ref, o_ref, w1bf_ref, w3bf_ref = refs
    m = pl.program_id(1)
    fresh = (fresh_ref[m] == 1) if grouped else (m == 0)

    @pl.when(fresh)
    def _():
        w1bf_ref[...] = w1_ref[0].astype(BF16)
        w3bf_ref[...] = w3_ref[0].astype(BF16)

    def compute():
        a = a_ref[...]
        h1 = jnp.dot(a, w1bf_ref[...], preferred_element_type=F32)
        h3 = jnp.dot(a, w3bf_ref[...], preferred_element_type=F32)
        o_ref[...] = (_silu(h1) * h3).astype(o_ref.dtype)

    if grouped:
        pl.when(live_ref[m] == 1)(compute)

        @pl.when(live_ref[m] == 0)
        def _():
            o_ref[...] = jnp.zeros_like(o_ref)
    else:
        compute()


def _swiglu_in(a, w1, w3, wl, *, tm, tn, tile_tables=None):
    M, K = a.shape
    N = w1.shape[2]
    n_tiles = pl.cdiv(N, tn)
    grouped = tile_tables is not None
    if grouped:
        a_map = lambda n, m, te, tf, tl, tt: (tt[m], 0)
        w_map = lambda n, m, te, tf, tl, tt: (te[m], 0, n)
        o_map = lambda n, m, te, tf, tl, tt: (m, n)
    else:
        a_map = lambda n, m: (m, 0)
        w_map = lambda n, m: (wl, 0, n)
        o_map = lambda n, m: (m, n)
    in_specs = [pl.BlockSpec((tm, K), a_map), pl.BlockSpec((1, K, tn), w_map), pl.BlockSpec((1, K, tn), w_map)]
    out_spec = pl.BlockSpec((tm, tn), o_map)
    scratch = [pltpu.VMEM((K, tn), BF16), pltpu.VMEM((K, tn), BF16)]
    vmem = 2 * tm * K * 2 + 4 * K * tn * 4 + 2 * K * tn * 2 + 2 * tm * tn * 2 + 6 * tm * tn * 4
    kern = functools.partial(_swiglu_kernel, grouped=grouped)
    out_shape = jax.ShapeDtypeStruct((M, N), BF16)
    cp = _params(("arbitrary", "arbitrary"), vmem)
    if grouped:
        gs = pltpu.PrefetchScalarGridSpec(num_scalar_prefetch=4, grid=(n_tiles, M // tm), in_specs=in_specs,
                                          out_specs=out_spec, scratch_shapes=scratch)
        return pl.pallas_call(kern, grid_spec=gs, out_shape=out_shape, compiler_params=cp,
                              name="moe_swiglu_in")(*tile_tables, a, w1, w3)
    return pl.pallas_call(kern, grid=(n_tiles, M // tm), in_specs=in_specs, out_specs=out_spec,
                          out_shape=out_shape, scratch_shapes=scratch, compiler_params=cp,
                          name="swiglu_in")(a, w1, w3)


def _expert_out_kernel(te_ref, fresh_ref, live_ref, tt_ref, a_ref, w_ref, o_ref):
    m, k = pl.program_id(0), pl.program_id(1)

    @pl.when(live_ref[m] == 1)
    def _():
        part = jnp.dot(a_ref[...], w_ref[0].astype(BF16), preferred_element_type=F32)

        @pl.when(k == 0)
        def _():
            o_ref[...] = part

        @pl.when(k != 0)
        def _():
            o_ref[...] += part

    @pl.when((live_ref[m] == 0) & (k == 0))
    def _():
        o_ref[...] = jnp.zeros_like(o_ref)


def _expert_out(h, w2, tile_tables, *, tm, tk):
    S, F = h.shape
    N = w2.shape[2]
    n_k = F // tk
    k_eff = lambda m, k, tl: jnp.where(tl[m] == 1, k, n_k - 1)
    gs = pltpu.PrefetchScalarGridSpec(
        num_scalar_prefetch=4, grid=(S // tm, n_k),
        in_specs=[pl.BlockSpec((tm, tk), lambda m, k, te, tf, tl, tt: (tt[m], k_eff(m, k, tl))),
                  pl.BlockSpec((1, tk, N), lambda m, k, te, tf, tl, tt: (te[m], k_eff(m, k, tl), 0))],
        out_specs=pl.BlockSpec((tm, N), lambda m, k, te, tf, tl, tt: (m, 0)))
    vmem = 2 * tm * tk * 2 + 2 * tk * N * 4 + tk * N * 2 + 2 * tm * N * 4 + 2 * tm * N * 4
    return pl.pallas_call(_expert_out_kernel, grid_spec=gs, out_shape=jax.ShapeDtypeStruct((S, N), F32),
                          compiler_params=_params(("arbitrary", "arbitrary"), vmem),
                          name="moe_expert_out")(*tile_tables, h, w2)


def _gdn_gates_kernel(ba_ref, alog_ref, dtb_ref, beta_ref, gc_ref, *, chunk, t_valid, n_heads):
    x = ba_ref[...]
    tc = x.shape[0]
    row = lax.broadcasted_iota(I32, (tc, tc), 0)
    col = lax.broadcasted_iota(I32, (tc, tc), 1)
    valid = (lax.broadcasted_iota(I32, x.shape, 0) % chunk) < t_valid
    beta_ref[...] = jnp.where(valid, 1.0 / (1.0 + jnp.exp(-x)), 0.0)
    y = x + dtb_ref[...]
    softplus = jnp.maximum(y, 0.0) + jnp.log1p(jnp.exp(-jnp.abs(y)))
    g = jnp.where(valid, -jnp.exp(alog_ref[...]) * softplus, 0.0)
    same_chunk_lower = ((row // chunk) == (col // chunk)) & (row >= col)
    gc_ref[...] = jnp.dot(jnp.where(same_chunk_lower, 1.0, 0.0), g, preferred_element_type=F32,
                          precision=HIGHEST)


def _gdn_gates(ba, a_log, dt_bias, *, t_valid, n_heads, tc):
    M = ba.shape[0]
    pad = lambda v: jnp.zeros((1, LANES), F32).at[0, n_heads:2 * n_heads].set(v.astype(F32))
    spec = pl.BlockSpec((tc, LANES), lambda i: (i, 0))
    vec = pl.BlockSpec((1, LANES), lambda i: (0, 0))
    return pl.pallas_call(
        functools.partial(_gdn_gates_kernel, chunk=GDN_CHUNK, t_valid=t_valid, n_heads=n_heads),
        grid=(M // tc,), in_specs=[spec, vec, vec], out_specs=[spec, spec],
        out_shape=[jax.ShapeDtypeStruct((M, LANES), F32)] * 2,
        compiler_params=_params(("arbitrary",), 0), name="gdn_gates")(ba, pad(a_log), pad(dt_bias))


def _gdn_kernel(q_ref, k_ref, v_ref, z_ref, cq_ref, ck_ref, cv_ref, wq_ref, wk_ref, wv_ref,
                bcol_ref, gcol_ref, grow_ref, s0_ref, onorm_ref, o_ref, sfin_ref,
                xq_ref, xk_ref, xv_ref, s_ref, *, hb, n_taps, dk):
    c = pl.program_id(2)
    C = q_ref.shape[1]
    lead = xq_ref.shape[1] - C

    @pl.when(c == 0)
    def _():
        xq_ref[:, 0:lead, :] = cq_ref[:, 0]
        xk_ref[:, 0:lead, :] = ck_ref[:, 0]
        xv_ref[:, 0:lead, :] = cv_ref[:, 0]
        s_ref[...] = s0_ref[0]

    row = lax.broadcasted_iota(I32, (C, C), 0)
    col = lax.broadcasted_iota(I32, (C, C), 1)
    tri, stri = row >= col, row > col
    eye = jnp.where(row == col, 1.0, 0.0)
    mm = functools.partial(jnp.dot, preferred_element_type=F32, precision=HIGHEST)

    for j in range(hb):
        def conv(x_ref, src_ref, w_ref):
            x_ref[j, lead:lead + C, :] = src_ref[j]
            y = x_ref[j, lead - n_taps + 1:lead - n_taps + 1 + C, :] * w_ref[j, 0:1, :]
            for t in range(1, n_taps):
                y = y + x_ref[j, lead - n_taps + 1 + t:lead - n_taps + 1 + t + C, :] * w_ref[j, t:t + 1, :]
            x_ref[j, 0:lead, :] = x_ref[j, C:C + lead, :]
            return _silu(y)

        q = conv(xq_ref, q_ref, wq_ref)
        k = conv(xk_ref, k_ref, wk_ref)
        v = conv(xv_ref, v_ref, wv_ref)
        q = q * lax.rsqrt(jnp.sum(q * q, axis=-1, keepdims=True) + NORM_EPS) * (dk ** -0.5)
        k = k * lax.rsqrt(jnp.sum(k * k, axis=-1, keepdims=True) + NORM_EPS)
        beta = bcol_ref[0][:, j:j + 1]
        gc = gcol_ref[0][:, j:j + 1]
        gr = grow_ref[0, 0][j:j + 1, :]
        decay = jnp.where(tri, jnp.exp(jnp.where(tri, gc - gr, 0.0)), 0.0)
        kb = k * beta
        a_mat = jnp.where(stri, _dot_nt(kb, k, HIGHEST) * decay, 0.0)
        inv = eye - a_mat
        pw = a_mat
        for _ in range(max(C.bit_length() - 2, 0)):
            pw = mm(pw, pw)
            inv = inv + mm(inv, pw)
        eg = jnp.exp(gc)
        u = mm(inv, v * beta)
        w = mm(inv, kb * eg)
        attn = jnp.where(tri, _dot_nt(q, k, HIGHEST) * decay, 0.0)
        s = s_ref[j]
        v_new = u - mm(w, s)
        o = mm(q * eg, s) + mm(attn, v_new)
        g_last = gc[C - 1:C, :]
        s_new = s * jnp.exp(g_last) + _dot_tn(k * jnp.exp(g_last - gc), v_new, HIGHEST)
        s_ref[j] = s_new
        sfin_ref[0, j] = s_new
        o = o * lax.rsqrt(jnp.mean(o * o, axis=-1, keepdims=True) + NORM_EPS) * onorm_ref[...]
        o_ref[j] = (o * _silu(z_ref[j])).astype(o_ref.dtype)


def _gdn_core(proj_hm, ba, conv_buf, s0, conv_w, a_log, dt_bias, o_norm, *, batch, n_chunks, t_valid, hb):
    H = s0.shape[1]
    dk = s0.shape[2]
    C = GDN_CHUNK
    M = proj_hm.shape[1]
    n_taps = conv_w.shape[0]
    lead = 8
    nhb = H // hb
    beta, gc = _gdn_gates(ba, a_log, dt_bias, t_valid=t_valid, n_heads=H, tc=min(M, 256))
    beta_col = beta[:, :H].reshape(M, nhb, hb).transpose(1, 0, 2)
    gc_col = gc[:, H:2 * H].reshape(M, nhb, hb).transpose(1, 0, 2)
    gc_row = gc[:, H:2 * H].reshape(M // C, C, nhb, hb).transpose(2, 0, 3, 1)
    cb = conv_buf.astype(F32).reshape(batch, n_taps - 1, 3 * H, LANES).transpose(2, 0, 1, 3)
    cb = jnp.pad(cb, ((0, 0), (0, 0), (lead - (n_taps - 1), 0), (0, 0)))
    cw = jnp.pad(conv_w.astype(F32).reshape(n_taps, 3 * H, LANES).transpose(1, 0, 2),
                 ((0, 0), (0, 8 - n_taps), (0, 0)))

    def tile_spec(section):
        return pl.BlockSpec((hb, C, LANES), lambda b, h, c: (section * nhb + h, b * n_chunks + c, 0))

    def cb_spec(section):
        return pl.BlockSpec((hb, 1, lead, LANES), lambda b, h, c: (section * nhb + h, b, 0, 0))

    def cw_spec(section):
        return pl.BlockSpec((hb, 8, LANES), lambda b, h, c: (section * nhb + h, 0, 0))

    col_spec = pl.BlockSpec((1, C, hb), lambda b, h, c: (h, b * n_chunks + c, 0))
    in_specs = [tile_spec(0), tile_spec(1), tile_spec(2), tile_spec(3), cb_spec(0), cb_spec(1), cb_spec(2),
                cw_spec(0), cw_spec(1), cw_spec(2), col_spec, col_spec,
                pl.BlockSpec((1, 1, hb, C), lambda b, h, c: (h, b * n_chunks + c, 0, 0)),
                pl.BlockSpec((1, hb, dk, LANES), lambda b, h, c: (b, h, 0, 0)),
                pl.BlockSpec((1, LANES), lambda b, h, c: (0, 0))]
    out_specs = [pl.BlockSpec((hb, C, LANES), lambda b, h, c: (h, b * n_chunks + c, 0)),
                 pl.BlockSpec((1, hb, dk, LANES), lambda b, h, c: (b, h, 0, 0))]
    out_shape = [jax.ShapeDtypeStruct((H, M, LANES), BF16), jax.ShapeDtypeStruct(s0.shape, F32)]
    scratch = [pltpu.VMEM((hb, C + lead, LANES), F32)] * 3 + [pltpu.VMEM((hb, dk, LANES), F32)]
    return pl.pallas_call(
        functools.partial(_gdn_kernel, hb=hb, n_taps=n_taps, dk=dk),
        grid=(batch, nhb, n_chunks), in_specs=in_specs, out_specs=out_specs, out_shape=out_shape,
        scratch_shapes=scratch, compiler_params=_params(("arbitrary",) * 3, 0), name="gdn_delta")(
            proj_hm, proj_hm, proj_hm, proj_hm, cb, cb, cb, cw, cw, cw, beta_col, gc_col, gc_row,
            s0.astype(F32), o_norm.reshape(1, LANES).astype(F32))


def _head_major_to_rows(x_hm):
    H, M, _ = x_hm.shape
    return x_hm.transpose(1, 0, 2).reshape(M, H * LANES)


def _gdn_mixer(h, x, gate, conv_buf, s0, w_in, w_ba, conv_w, a_log, dt_bias, o_norm, w_out, *,
               batch, seq, rows_per_group, tm):
    H = s0.shape[1]
    n_taps = conv_w.shape[0]
    C = GDN_CHUNK
    M = batch * seq
    qkvz = 4 * H * LANES
    proj_hm = _matmul(h, w_in, 0, ncols=qkvz, tm=tm, tn=512, head_major=True)
    ba = _matmul(h, w_ba, 0, tm=tm, tn=LANES)
    u_rows = proj_hm[:3 * H].reshape(3 * H, batch, seq, LANES)
    if seq % C == 0:
        n_chunks, t_valid, padded = seq // C, C, proj_hm
        ba_p = ba
    else:
        n_chunks, t_valid = 1, seq
        padded = jnp.pad(proj_hm.reshape(4 * H, batch, seq, LANES), ((0, 0), (0, 0), (0, C - seq), (0, 0)))
        padded = padded.reshape(4 * H, batch * C, LANES)
        ba_p = jnp.pad(ba.reshape(batch, seq, LANES), ((0, 0), (0, C - seq), (0, 0))).reshape(batch * C, LANES)
    o_hm, s_fin = _gdn_core(padded, ba_p, conv_buf, s0, conv_w, a_log, dt_bias, o_norm, batch=batch,
                            n_chunks=n_chunks, t_valid=t_valid, hb=4)
    o_hm = o_hm.reshape(H, batch, n_chunks * C, LANES)[:, :, :seq].reshape(H, M, LANES)
    o_rows = _head_major_to_rows(o_hm)
    x_new = _matmul(o_rows, w_out, 0, tm=tm, tn=512, resid=x, gate=gate, rows_per_group=rows_per_group)
    keep = min(seq, n_taps - 1)
    new_buf = u_rows[:, :, seq - keep:].transpose(1, 2, 0, 3).reshape(batch, keep, 3 * H * LANES)
    if keep < n_taps - 1:
        new_buf = jnp.concatenate([conv_buf.astype(F32)[:, keep:], new_buf], axis=1)
    return x_new, new_buf, s_fin


def _ordered_key(score):
    bits = pltpu.bitcast(score + 0.0, I32)
    return bits ^ ((bits >> 31) & 0x7FFFFFFF)


def _topk_select(keys, col, k):
    count = lambda m: jnp.sum(m.astype(I32), axis=1, keepdims=True)
    n_rows, n_cols = keys.shape
    zero = jnp.zeros((n_rows, 1), I32)
    thr = jnp.where(count(keys >= zero) >= k, zero, jnp.full((n_rows, 1), INT32_MIN, I32))

    def bit_step(i, thr):
        cand = thr | (1 << (30 - i))
        return jnp.where(count(keys >= cand) >= k, cand, thr)

    thr = lax.fori_loop(0, 31, bit_step, thr)
    above = keys > thr
    tied = keys == thr
    need = k - count(above)
    n_bits = n_cols.bit_length()

    def tie_step(i, bound):
        cand = bound + (1 << (n_bits - 1 - i))
        ok = (cand <= n_cols) & (count(tied & (col < cand)) <= need)
        return jnp.where(ok, cand, bound)

    bound = lax.fori_loop(0, n_bits, tie_step, zero)
    return above | (tied & (col < bound))


def _dsa_prompt_kernel(qi_ref, wi_ref, ki_ref, q_ref, k_ref, v_ref, o_ref, sc_ref, bias_ref, *,
                       hc, idx_scale, att_scale, topk, group):
    i = pl.program_id(1)
    n_q, S = sc_ref.shape
    ki = ki_ref[0]
    n_chunks = qi_ref.shape[0] // hc
    sc_ref[...] = jnp.zeros_like(sc_ref)

    def chunk(c, carry):
        lhs = qi_ref[pl.ds(c * hc, hc)].reshape(hc * n_q, qi_ref.shape[2])
        d = jnp.maximum(_dot_nt(lhs, ki), 0.0).reshape(hc, n_q, S)
        w = wi_ref[c]
        acc = d[0] * w[:, 0:1]
        for j in range(1, hc):
            acc = acc + d[j] * w[:, j:j + 1]
        sc_ref[...] += acc
        return carry

    lax.fori_loop(0, n_chunks, chunk, 0)
    col = lax.broadcasted_iota(I32, (n_q, S), 1)
    q_pos = i * n_q + lax.broadcasted_iota(I32, (n_q, S), 0)
    allowed = col <= q_pos
    keys = jnp.where(allowed, _ordered_key(sc_ref[...] * idx_scale), INT32_MIN)
    sel = _topk_select(keys, col, topk) & allowed
    bias_ref[...] = jnp.where(sel, 0.0, -jnp.inf)

    def grp(g, carry):
        qg = q_ref[pl.ds(g * group, group)].reshape(group * n_q, q_ref.shape[2])
        s = _dot_nt(qg, k_ref[0, g]) * att_scale
        s = s.reshape(group, n_q, S) + bias_ref[...][None]
        p = jnp.exp(s - jnp.max(s, axis=-1, keepdims=True))
        l = jnp.sum(p, axis=-1, keepdims=True)
        o = jnp.dot(p.reshape(group * n_q, S).astype(BF16), v_ref[0, g], preferred_element_type=F32)
        o = o.reshape(group, n_q, o.shape[-1]) / l
        o_ref[pl.ds(g * group, group)] = o.astype(o_ref.dtype)
        return carry

    lax.fori_loop(0, k_ref.shape[1], grp, 0)


def _dsa_prompt(q_hm, qi_hm, wi, ki, k_hm, v_hm, *, batch, seq, topk):
    HQ, M, dh = q_hm.shape
    HI = qi_hm.shape[0]
    G = k_hm.shape[1]
    nq = ATT_BLOCK
    nb = seq // nq
    hc = 4
    wi_c = wi.reshape(M, HI // hc, hc).transpose(1, 0, 2)
    di = qi_hm.shape[2]
    kern = functools.partial(_dsa_prompt_kernel, hc=hc, idx_scale=float(di ** -0.5 * HI ** -0.5),
                             att_scale=float(dh ** -0.5), topk=topk, group=HQ // G)
    vmem = (2 * HI * nq * di * 2 + 2 * HQ * nq * dh * 2 * 2 + 4 * G * seq * dh * 2 * 2 + 2 * seq * di * 2
            + 2 * nq * seq * 4 + 3 * hc * nq * seq * 4 + 4 * (HQ // G) * nq * seq * 4 + (8 << 20))
    return pl.pallas_call(
        kern, grid=(batch, nb),
        in_specs=[pl.BlockSpec((HI, nq, di), lambda b, i: (0, b * nb + i, 0)),
                  pl.BlockSpec((HI // hc, nq, hc), lambda b, i: (0, b * nb + i, 0)),
                  pl.BlockSpec((1, seq, di), lambda b, i: (b, 0, 0)),
                  pl.BlockSpec((HQ, nq, dh), lambda b, i: (0, b * nb + i, 0)),
                  pl.BlockSpec((1, G, seq, dh), lambda b, i: (b, 0, 0, 0)),
                  pl.BlockSpec((1, G, seq, dh), lambda b, i: (b, 0, 0, 0))],
        out_specs=pl.BlockSpec((HQ, nq, dh), lambda b, i: (0, b * nb + i, 0)),
        out_shape=jax.ShapeDtypeStruct((HQ, M, dh), BF16),
        scratch_shapes=[pltpu.VMEM((nq, seq), F32), pltpu.VMEM((nq, seq), F32)],
        compiler_params=_params(("arbitrary", "arbitrary"), vmem), name="dsa_prompt")(
            qi_hm, wi_c, ki, q_hm, k_hm, v_hm)


def _dsa_sample_scores_kernel(pt_ref, qi_ref, wi_ref, pool_ref, new_ref, o_ref, *, n_pages, n_tok, idx_scale):
    p = pl.program_id(1)
    page = jnp.where(p < n_pages, pool_ref[0], new_ref[0]).astype(BF16)
    d = jnp.maximum(_dot_nt(qi_ref[0], page), 0.0) * wi_ref[0]
    hi = d.shape[0] // n_tok
    o_ref[0] = jnp.sum(d.reshape(n_tok, hi, d.shape[1]), axis=1) * idx_scale


def _dsa_sample_select_kernel(sc_ref, o_ref, *, n_past, topk):
    sc = sc_ref[0]
    col = lax.broadcasted_iota(I32, sc.shape, 1)
    q_pos = n_past + lax.broadcasted_iota(I32, sc.shape, 0)
    allowed = col <= q_pos
    keys = jnp.where(allowed, _ordered_key(sc), INT32_MIN)
    sel = _topk_select(keys, col, topk) & allowed
    o_ref[0] = jnp.where(sel, 0.0, -jnp.inf)


def _dsa_sample_attn_kernel(pt_ref, q_ref, bias_ref, kp_ref, vp_ref, kn_ref, vn_ref, o_ref,
                            m_ref, l_ref, acc_ref, *, n_pages, att_scale, n_rep):
    p = pl.program_id(1)
    G = q_ref.shape[1]

    @pl.when(p == 0)
    def _():
        m_ref[...] = jnp.full_like(m_ref, -jnp.inf)
        l_ref[...] = jnp.zeros_like(l_ref)
        acc_ref[...] = jnp.zeros_like(acc_ref)

    past = p < n_pages
    bias = jnp.concatenate([bias_ref[0]] * n_rep, axis=0)
    for g in range(G):
        kg = jnp.where(past, kp_ref[0, :, g, :], kn_ref[0, :, g, :]).astype(BF16)
        vg = jnp.where(past, vp_ref[0, :, g, :], vn_ref[0, :, g, :]).astype(BF16)
        s = _dot_nt(q_ref[0, g], kg) * att_scale + bias
        m_old = m_ref[g]
        m_new = jnp.maximum(m_old, jnp.max(s, axis=-1, keepdims=True))
        m_safe = jnp.where(m_new == -jnp.inf, 0.0, m_new)
        alpha = jnp.exp(m_old - m_safe)
        pr = jnp.exp(s - m_safe)
        l_ref[g] = alpha * l_ref[g] + jnp.sum(pr, axis=-1, keepdims=True)
        acc_ref[g] = alpha * acc_ref[g] + jnp.dot(pr.astype(BF16), vg, preferred_element_type=F32)
        m_ref[g] = m_new

    @pl.when(p == pl.num_programs(1) - 1)
    def _():
        o_ref[0] = (acc_ref[...] / l_ref[...]).astype(o_ref.dtype)


def _dsa_sample(q, qi, wi, ki_new, k_new, v_new, k_pool, v_pool, ki_pool, page_table, *, topk):
    B, T, _ = q.shape
    n_pool, page, G, dh = k_pool.shape
    di = ki_pool.shape[2]
    HI = wi.shape[2]
    HQ = q.shape[2] // dh
    n_rep = HQ // G
    P = page_table.shape[1]
    n_past = P * page
    s_pad = (P + 1) * page
    idx_scale = float(di ** -0.5 * HI ** -0.5)
    pad_rows = lambda a: jnp.pad(a, ((0, 0), (0, page - T)) + ((0, 0),) * (a.ndim - 2))
    last = lambda b, p, pt: pt[b, jnp.minimum(p, P - 1)]

    scores = pl.pallas_call(
        functools.partial(_dsa_sample_scores_kernel, n_pages=P, n_tok=T, idx_scale=idx_scale),
        grid_spec=pltpu.PrefetchScalarGridSpec(
            num_scalar_prefetch=1, grid=(B, P + 1),
            in_specs=[pl.BlockSpec((1, T * HI, di), lambda b, p, pt: (b, 0, 0)),
                      pl.BlockSpec((1, T * HI, 1), lambda b, p, pt: (b, 0, 0)),
                      pl.BlockSpec((1, page, di), lambda b, p, pt: (last(b, p, pt), 0, 0)),
                      pl.BlockSpec((1, page, di), lambda b, p, pt: (b, 0, 0))],
            out_specs=pl.BlockSpec((1, T, page), lambda b, p, pt: (b, 0, p))),
        out_shape=jax.ShapeDtypeStruct((B, T, s_pad), F32),
        compiler_params=_params(("arbitrary", "arbitrary"), 0), name="dsa_sample_scores")(
            page_table, qi.reshape(B, T * HI, di), wi.reshape(B, T * HI, 1), ki_pool, pad_rows(ki_new))

    bias = pl.pallas_call(
        functools.partial(_dsa_sample_select_kernel, n_past=n_past, topk=topk),
        grid=(B,), in_specs=[pl.BlockSpec((1, T, s_pad), lambda b: (b, 0, 0))],
        out_specs=pl.BlockSpec((1, T, s_pad), lambda b: (b, 0, 0)),
        out_shape=jax.ShapeDtypeStruct((B, T, s_pad), F32),
        compiler_params=_params(("arbitrary",), 0), name="dsa_sample_select")(scores)

    rows = n_rep * T
    q_g = q.reshape(B, T, G, n_rep, dh).transpose(0, 2, 3, 1, 4).reshape(B, G, rows, dh)
    o = pl.pallas_call(
        functools.partial(_dsa_sample_attn_kernel, n_pages=P, att_scale=float(dh ** -0.5), n_rep=n_rep),
        grid_spec=pltpu.PrefetchScalarGridSpec(
            num_scalar_prefetch=1, grid=(B, P + 1),
            in_specs=[pl.BlockSpec((1, G, rows, dh), lambda b, p, pt: (b, 0, 0, 0)),
                      pl.BlockSpec((1, T, page), lambda b, p, pt: (b, 0, p)),
                      pl.BlockSpec((1, page, G, dh), lambda b, p, pt: (last(b, p, pt), 0, 0, 0)),
                      pl.BlockSpec((1, page, G, dh), lambda b, p, pt: (last(b, p, pt), 0, 0, 0)),
                      pl.BlockSpec((1, page, G, dh), lambda b, p, pt: (b, 0, 0, 0)),
                      pl.BlockSpec((1, page, G, dh), lambda b, p, pt: (b, 0, 0, 0))],
            out_specs=pl.BlockSpec((1, G, rows, dh), lambda b, p, pt: (b, 0, 0, 0)),
            scratch_shapes=[pltpu.VMEM((G, rows, 1), F32), pltpu.VMEM((G, rows, 1), F32),
                            pltpu.VMEM((G, rows, dh), F32)]),
        out_shape=jax.ShapeDtypeStruct((B, G, rows, dh), BF16),
        compiler_params=_params(("arbitrary", "arbitrary"), 0), name="dsa_sample_attn")(
            page_table, q_g, bias, k_pool, v_pool, pad_rows(k_new), pad_rows(v_new))
    return o.reshape(B, G, n_rep, T, dh).transpose(0, 3, 1, 2, 4).reshape(B, T, HQ * dh)


def _gather_rows_kernel(idx_ref, x_hbm, o_ref, sem, *, tm):
    base = pl.program_id(0) * tm

    def copy(r):
        return pltpu.make_async_copy(x_hbm.at[pl.ds(idx_ref[base + r], 1)], o_ref.at[pl.ds(r, 1)], sem)

    def start(r, c):
        copy(r).start()
        return c

    def wait(r, c):
        copy(r).wait()
        return c

    lax.fori_loop(0, tm, start, 0)
    lax.fori_loop(0, tm, wait, 0)


def _gather_rows(x, idx, *, tm):
    S = idx.shape[0]
    W = x.shape[1]
    return pl.pallas_call(
        functools.partial(_gather_rows_kernel, tm=tm),
        grid_spec=pltpu.PrefetchScalarGridSpec(
            num_scalar_prefetch=1, grid=(S // tm,),
            in_specs=[pl.BlockSpec(memory_space=pl.ANY)],
            out_specs=pl.BlockSpec((tm, W), lambda i, idx: (i, 0)),
            scratch_shapes=[pltpu.SemaphoreType.DMA]),
        out_shape=jax.ShapeDtypeStruct((S, W), x.dtype),
        compiler_params=_params(("arbitrary",), 4 * tm * W * 4), name="moe_gather")(idx, x)


def _combine_kernel(slot_ref, y_hbm, x_ref, gt_ref, rg_ref, fg_ref, o_ref, buf_ref, sem, *, tm, n_top, row0):
    base = row0 + pl.program_id(0) * tm

    def copy(r, j):
        return pltpu.make_async_copy(y_hbm.at[pl.ds(slot_ref[(base + r) * n_top + j], 1)],
                                     buf_ref.at[j, pl.ds(r, 1)], sem)

    def start(r, c):
        for j in range(n_top):
            copy(r, j).start()
        return c

    def wait(r, c):
        for j in range(n_top):
            copy(r, j).wait()
        return c

    lax.fori_loop(0, tm, start, 0)
    lax.fori_loop(0, tm, wait, 0)
    rg = rg_ref[...]
    moe = buf_ref[0] * rg[:, 0:1]
    for j in range(1, n_top):
        moe = moe + buf_ref[j] * rg[:, j:j + 1]
    x = x_ref[...] + gt_ref[0] * moe
    o_ref[...] = x * lax.rsqrt(jnp.mean(x * x, axis=-1, keepdims=True) + NORM_EPS) * fg_ref[...]


def _combine(y_slots, slot_of, x, gate, route_gate, final_g, *, rows_per_group, tm, row0):
    M, D = x.shape
    tiles_per_group = rows_per_group // tm
    t0 = row0 // tm
    return pl.pallas_call(
        functools.partial(_combine_kernel, tm=tm, n_top=MOE_TOPK, row0=row0),
        grid_spec=pltpu.PrefetchScalarGridSpec(
            num_scalar_prefetch=1, grid=(M // tm,),
            in_specs=[pl.BlockSpec(memory_space=pl.ANY),
                      pl.BlockSpec((tm, D), lambda i, s: (i, 0)),
                      pl.BlockSpec((1, gate.shape[1], D), lambda i, s: (i // tiles_per_group, 0, 0)),
                      pl.BlockSpec((tm, LANES), lambda i, s: (i + t0, 0)),
                      pl.BlockSpec((1, D), lambda i, s: (0, 0))],
            out_specs=pl.BlockSpec((tm, D), lambda i, s: (i, 0)),
            scratch_shapes=[pltpu.VMEM((MOE_TOPK, tm, D), F32), pltpu.SemaphoreType.DMA]),
        out_shape=jax.ShapeDtypeStruct((M, D), F32),
        compiler_params=_params(("arbitrary",), (MOE_TOPK + 8) * tm * D * 4), name="moe_combine")(
            slot_of, y_slots, x, gate, route_gate, final_g.reshape(1, D))


def _route_tables(expert_idx, n_experts, tm):
    N, top = expert_idx.shape
    flat = expert_idx.reshape(-1)
    onehot = (flat[:, None] == jnp.arange(n_experts, dtype=I32)[None, :]).astype(I32)
    rank = jnp.take_along_axis(jnp.cumsum(onehot, axis=0) - 1, flat[:, None], axis=1)[:, 0]
    counts = jnp.sum(onehot, axis=0)
    tiles = (counts + tm - 1) // tm
    tile_end = jnp.cumsum(tiles)
    tile_start = tile_end - tiles
    n_tiles = (N * top + n_experts * (tm - 1)) // tm
    slot_of = tile_start[flat] * tm + rank
    src_row = jnp.zeros((n_tiles * tm,), I32).at[slot_of].set(jnp.arange(N * top, dtype=I32) // top)
    t = jnp.arange(n_tiles, dtype=I32)
    live = t < tile_end[-1]
    last_live = jnp.maximum(tile_end[-1] - 1, 0)
    t_eff = jnp.where(live, t, last_live)
    expert = jnp.minimum(jnp.searchsorted(tile_end, t_eff, side="right"), n_experts - 1).astype(I32)
    fresh = jnp.concatenate([jnp.ones((1,), I32), (expert[1:] != expert[:-1]).astype(I32)])
    return src_row, slot_of.astype(I32), (expert, fresh, live.astype(I32), t_eff.astype(I32))


def _moe(h_rows, expert_idx, w1, w3, w2, *, tm):
    N, D = h_rows.shape
    src_row, slot_of, tables = _route_tables(expert_idx, w1.shape[0], tm)
    per_word = 4 // h_rows.dtype.itemsize
    packed = lax.bitcast_convert_type(h_rows.reshape(N, D // per_word, per_word), jnp.uint32)
    xg = _gather_rows(packed.reshape(N, D // per_word), src_row, tm=tm)
    xg = lax.bitcast_convert_type(xg, h_rows.dtype).reshape(-1, D)
    hmid = _swiglu_in(xg, w1, w3, None, tm=tm, tn=512, tile_tables=tables)
    y = _expert_out(hmid, w2, tables, tm=tm, tk=512)
    return y, slot_of


def _pad_cols(w, width):
    return jnp.pad(w, ((0, 0), (0, 0), (0, width - w.shape[2])))


def kernel(x_prompt, x_sample, state_gdn, state_conv, cache_k, cache_v, cache_kidx, page_table, c_prompt, c_sample, ada_w, ada_b, norm1_g, norm2_g, gdn_w_in, gdn_conv_w, gdn_a_log, gdn_dt_bias, gdn_o_norm, gdn_w_out, att_w_in, att_w_out, ffn_w1, ffn_w3, ffn_w2, moe_router_w, moe_w1, moe_w3, moe_w2, final_norm_g):
    BP, TP, D = x_prompt.shape
    BS, TS, _ = x_sample.shape
    MP, MS = BP * TP, BS * TS
    H = state_gdn.shape[2]
    G, dh = cache_k.shape[3], cache_k.shape[4]
    di = cache_kidx.shape[3]
    HQ = att_w_out.shape[1] // dh
    q_w, kv_w = HQ * dh, G * dh
    HI = (att_w_in.shape[2] - q_w - 2 * kv_w - di) // (di + 1)
    qi_w = HI * di
    n_experts = moe_w1.shape[1]
    tmp = 512 if MP % 512 == 0 else MP
    tms = MS

    n_c = BP + BS
    c_all = jnp.pad(jnp.concatenate([c_prompt, c_sample], axis=0), ((0, -n_c % 16), (0, 0)))
    mods = [_matmul(c_all, ada_w, l, tm=c_all.shape[0], tn=512, a_silu=True, bias=ada_b.reshape(-1, 1, 6 * D))
            for l in range(ada_w.shape[0])]

    def mod_p(l, j):
        return mods[l][:BP, j * D:(j + 1) * D].reshape(BP, 1, D)

    def mod_s(l, j):
        return jnp.repeat(mods[l][BP:n_c, j * D:(j + 1) * D], TS, axis=0).reshape(1, MS, D)

    xp = x_prompt.reshape(MP, D)
    xs = x_sample.reshape(MS, D)

    hp = _modulate(xp, norm1_g[0], mod_p(0, 0), mod_p(0, 1), rows_per_group=TP, tm=256, out_dtype=BF16)
    hs = _modulate(xs, norm1_g[0], mod_s(0, 0), mod_s(0, 1), rows_per_group=MS, tm=tms, out_dtype=BF16)
    qkvz = 4 * H * LANES
    w_ba = _pad_cols(gdn_w_in[:, :, qkvz:], LANES)
    zeros_buf = jnp.zeros((BP,) + state_conv.shape[2:], F32)
    zeros_s = jnp.zeros((BP,) + state_gdn.shape[2:], F32)
    gdn = functools.partial(_gdn_mixer, w_in=gdn_w_in, w_ba=w_ba, conv_w=gdn_conv_w[0], a_log=gdn_a_log[0],
                            dt_bias=gdn_dt_bias[0], o_norm=gdn_o_norm[0], w_out=gdn_w_out)
    xp, conv_p, s_p = gdn(hp, xp, mod_p(0, 2), zeros_buf, zeros_s, batch=BP, seq=TP, rows_per_group=TP, tm=tmp)
    xs, conv_s, s_s = gdn(hs, xs, mod_s(0, 2), state_conv[0], state_gdn[0], batch=BS, seq=TS,
                          rows_per_group=MS, tm=tms)

    def ffn(x, shift, scale, gate, rows_per_group, tm, tmod):
        h = _modulate(x, norm2_g[0], shift, scale, rows_per_group=rows_per_group, tm=tmod, out_dtype=BF16)
        mid = _swiglu_in(h, ffn_w1, ffn_w3, 0, tm=tm, tn=512)
        return _matmul(mid, ffn_w2, 0, tm=min(tm, 256), tn=256, resid=x, gate=gate, rows_per_group=rows_per_group)

    xp = ffn(xp, mod_p(0, 3), mod_p(0, 4), mod_p(0, 5), TP, tmp, 256)
    xs = ffn(xs, mod_s(0, 3), mod_s(0, 4), mod_s(0, 5), MS, tms, tms)

    hp = _modulate(xp, norm1_g[1], mod_p(1, 0), mod_p(1, 1), rows_per_group=TP, tm=256, out_dtype=BF16)
    hs = _modulate(xs, norm1_g[1], mod_s(1, 0), mod_s(1, 1), rows_per_group=MS, tm=tms, out_dtype=BF16)
    w_kiwi = _pad_cols(att_w_in[:, :, q_w + 2 * kv_w + qi_w:], 2 * LANES)

    def att_proj(h, tm, head_major):
        q = _matmul(h, att_w_in, 0, col0=0, ncols=q_w, tm=tm, tn=512, out_dtype=BF16, head_major=head_major)
        kv = _matmul(h, att_w_in, 0, col0=q_w, ncols=2 * kv_w, tm=tm, tn=512)
        qi = _matmul(h, att_w_in, 0, col0=q_w + 2 * kv_w, ncols=qi_w, tm=tm, tn=512, out_dtype=BF16,
                     head_major=head_major)
        kiwi = _matmul(h, w_kiwi, 0, tm=tm, tn=2 * LANES)
        return q, kv[:, :kv_w], kv[:, kv_w:], qi, kiwi[:, :di], kiwi[:, di:di + HI]

    q, k_p, v_p, qi, ki_p, wi = att_proj(hp, tmp, True)
    to_hm = lambda a: a.astype(BF16).reshape(BP, TP, G, dh).transpose(0, 2, 1, 3)
    o_hm = _dsa_prompt(q, qi, wi, ki_p.astype(BF16).reshape(BP, TP, di), to_hm(k_p), to_hm(v_p),
                       batch=BP, seq=TP, topk=min(IDX_TOPK, TP // 4))
    xp = _matmul(_head_major_to_rows(o_hm), att_w_out, 0, tm=tmp, tn=512, resid=xp, gate=mod_p(1, 2),
                 rows_per_group=TP)

    q, k_s, v_s, qi, ki_s, wi = att_proj(hs, tms, False)
    n_past = page_table.shape[1] * cache_k.shape[2]
    o_s = _dsa_sample(q.reshape(BS, TS, q_w), qi.reshape(BS, TS, qi_w), wi.reshape(BS, TS, HI),
                      ki_s.reshape(BS, TS, di), k_s.reshape(BS, TS, G, dh), v_s.reshape(BS, TS, G, dh),
                      cache_k.reshape(cache_k.shape[1:]), cache_v.reshape(cache_v.shape[1:]),
                      cache_kidx.reshape(cache_kidx.shape[1:]), page_table,
                      topk=min(IDX_TOPK, (n_past + TS) // 4))
    xs = _matmul(o_s.reshape(MS, q_w), att_w_out, 0, tm=tms, tn=512, resid=xs, gate=mod_s(1, 2),
                 rows_per_group=MS)

    rw = jnp.pad(moe_router_w[0], ((0, 0), (0, LANES - n_experts)))
    hp, ridx_p, rgate_p = _modulate(xp, norm2_g[1], mod_p(1, 3), mod_p(1, 4), rows_per_group=TP, tm=256,
                                    out_dtype=BF16, router_w=rw, n_experts=n_experts)
    hs, ridx_s, rgate_s = _modulate(xs, norm2_g[1], mod_s(1, 3), mod_s(1, 4), rows_per_group=MS, tm=tms,
                                    out_dtype=BF16, router_w=rw, n_experts=n_experts)
    h_all = jnp.concatenate([hp, hs], axis=0)
    ridx = jnp.concatenate([ridx_p, ridx_s], axis=0)[:, :MOE_TOPK]
    rgate = jnp.concatenate([rgate_p, rgate_s], axis=0)
    drop_layer = lambda w: w.reshape(w.shape[1:])
    y_slots, slot_of = _moe(h_all, ridx, drop_layer(moe_w1), drop_layer(moe_w3), drop_layer(moe_w2),
                            tm=min(512, MP))
    y_prompt = _combine(y_slots, slot_of, xp, mod_p(1, 5), rgate, final_norm_g, rows_per_group=TP, tm=256, row0=0)
    y_sample = _combine(y_slots, slot_of, xs, mod_s(1, 5), rgate, final_norm_g, rows_per_group=MS, tm=tms,
                        row0=MP)

    f32 = lambda a: a.astype(F32)
    return (y_prompt.reshape(BP, TP, D), y_sample.reshape(BS, TS, D),
            s_p[None], conv_p[None], s_s[None], conv_s[None],
            f32(k_p).reshape(1, BP, TP, G, dh), f32(v_p).reshape(1, BP, TP, G, dh), f32(ki_p).reshape(1, BP, TP, di),
            f32(k_s).reshape(1, BS, TS, G, dh), f32(v_s).reshape(1, BS, TS, G, dh), f32(ki_s).reshape(1, BS, TS, di))
```

```python
import functools

import jax
import jax.numpy as jnp
from jax import lax
from jax.experimental import pallas as pl
from jax.experimental.pallas import tpu as pltpu

F32 = jnp.float32
BF16 = jnp.bfloat16
I32 = jnp.int32
HIGHEST = lax.Precision.HIGHEST

LANES = 128
VMEM_CAP_V7X = 60 * 1024 * 1024
NORM_EPS = 1e-6
GDN_CHUNK = 64
IDX_TOPK = 256
ATT_BLOCK = 128
MOE_TOPK = 2
INT32_MIN = -(2 ** 31)


def _params(semantics, vmem_bytes):
    limit = int(min(max(vmem_bytes, 32 * 1024 * 1024), VMEM_CAP_V7X))
    return pltpu.CompilerParams(dimension_semantics=semantics, vmem_limit_bytes=limit)


def _silu(x):
    return x / (1.0 + jnp.exp(-x))


def _dot_nt(a, b, precision=None):
    return lax.dot_general(a, b, (((1,), (1,)), ((), ())), preferred_element_type=F32, precision=precision)


def _dot_tn(a, b, precision=None):
    return lax.dot_general(a, b, (((0,), (0,)), ((), ())), preferred_element_type=F32, precision=precision)


def _bf(x):
    return x.astype(BF16)


def _hi_lo(x):
    hi = x.astype(BF16)
    return hi, (x - hi.astype(F32)).astype(BF16)


def _dot3(a_hi, a_lo, b_hi, b_lo, dot):
    return dot(a_hi, b_hi) + (dot(a_lo, b_hi) + dot(a_hi, b_lo))


def _modulate_kernel(*refs, modulated, n_experts):
    it = iter(refs)
    x_ref, g_ref = next(it), next(it)
    sh_ref = sc_ref = rw_ref = None
    if modulated:
        sh_ref, sc_ref = next(it), next(it)
    if n_experts:
        rw_ref = next(it)
    o_ref = next(it)
    x = x_ref[...]
    y = x * lax.rsqrt(jnp.mean(x * x, axis=-1, keepdims=True) + NORM_EPS) * g_ref[...]
    if modulated:
        y = y * (1.0 + sc_ref[0]) + sh_ref[0]
    o_ref[...] = y.astype(o_ref.dtype)
    if n_experts:
        ridx_ref, rgate_ref = next(it), next(it)
        logits = jnp.dot(y, rw_ref[...], preferred_element_type=F32, precision=HIGHEST)
        lane = lax.broadcasted_iota(I32, logits.shape, 1)
        l1 = jnp.where(lane < n_experts, logits, -jnp.inf)
        m1 = jnp.max(l1, axis=1, keepdims=True)
        i1 = jnp.min(jnp.where(l1 == m1, lane, LANES), axis=1, keepdims=True)
        l2 = jnp.where(lane == i1, -jnp.inf, l1)
        m2 = jnp.max(l2, axis=1, keepdims=True)
        i2 = jnp.min(jnp.where(l2 == m2, lane, LANES), axis=1, keepdims=True)
        e1 = jnp.exp(m2 - m1)
        den = 1.0 + e1
        ridx_ref[...] = jnp.where(lane == 0, i1, jnp.where(lane == 1, i2, 0))
        rgate_ref[...] = jnp.where(lane == 0, 1.0 / den, jnp.where(lane == 1, e1 / den, 0.0))


def _modulate(x, g, shift, scale, *, rows_per_group, tm, out_dtype, router_w=None, n_experts=0):
    M, D = x.shape
    modulated = shift is not None
    tiles_per_group = rows_per_group // tm
    in_specs = [pl.BlockSpec((tm, D), lambda i: (i, 0)), pl.BlockSpec((1, D), lambda i: (0, 0))]
    args = [x, g.reshape(1, D)]
    if modulated:
        R = shift.shape[1]
        spec = pl.BlockSpec((1, R, D), lambda i: (i // tiles_per_group, 0, 0))
        in_specs += [spec, spec]
        args += [shift, scale]
    out_shape = [jax.ShapeDtypeStruct((M, D), out_dtype)]
    out_specs = [pl.BlockSpec((tm, D), lambda i: (i, 0))]
    if n_experts:
        in_specs.append(pl.BlockSpec((D, LANES), lambda i: (0, 0)))
        args.append(router_w)
        out_shape += [jax.ShapeDtypeStruct((M, LANES), I32), jax.ShapeDtypeStruct((M, LANES), F32)]
        out_specs += [pl.BlockSpec((tm, LANES), lambda i: (i, 0))] * 2
    vmem = 2 * tm * D * (4 + jnp.dtype(out_dtype).itemsize) + 6 * tm * D * 4 + 16 * D * 4
    res = pl.pallas_call(
        functools.partial(_modulate_kernel, modulated=modulated, n_experts=n_experts),
        grid=(M // tm,), in_specs=in_specs, out_specs=out_specs, out_shape=out_shape,
        compiler_params=_params(("arbitrary",), vmem), name="modulate")(*args)
    return res if n_experts else res[0]


def _mm_kernel(*refs, a_silu, head_major, has_bias, has_resid, precise):
    it = iter(refs)
    a_ref, w_ref = next(it), next(it)
    b_ref = next(it) if has_bias else None
    r_ref, gt_ref = (next(it), next(it)) if has_resid else (None, None)
    o_ref = next(it)
    o2_ref = next(it) if head_major == "both" else None
    wbf_ref = next(it)
    wlo_ref = next(it) if precise else None

    @pl.when(pl.program_id(1) == 0)
    def _():
        if precise:
            wbf_ref[...], wlo_ref[...] = _hi_lo(w_ref[0])
        else:
            wbf_ref[...] = w_ref[0].astype(BF16)

    a = a_ref[...]
    if a_silu:
        a = _silu(a)
    dot = functools.partial(jnp.dot, preferred_element_type=F32)
    if precise:
        acc = _dot3(*_hi_lo(a), wbf_ref[...], wlo_ref[...], dot)
    else:
        acc = dot(a.astype(BF16), wbf_ref[...])
    if has_bias:
        acc = acc + b_ref[0]
    if has_resid:
        acc = r_ref[...] + gt_ref[0] * acc
    hm_ref = o2_ref if head_major == "both" else (o_ref if head_major else None)
    if hm_ref is not None:
        for j in range(hm_ref.shape[0]):
            hm_ref[j] = acc[:, j * LANES:(j + 1) * LANES].astype(hm_ref.dtype)
    if head_major is not True:
        o_ref[...] = acc.astype(o_ref.dtype)


def _matmul(a, w, wl, *, col0=0, ncols=None, tm, tn, out_dtype=F32, head_major=False, a_silu=False,
            bias=None, resid=None, gate=None, rows_per_group=None, k_parts=1, k_part=0, precise=False):
    M, K = a.shape
    K = K // k_parts
    ncols = w.shape[2] - col0 if ncols is None else ncols
    assert col0 % tn == 0 and M % tm == 0
    n_tiles, off = pl.cdiv(ncols, tn), col0 // tn
    in_specs = [pl.BlockSpec((tm, K), lambda n, m: (m, k_part)),
                pl.BlockSpec((1, K, tn), lambda n, m: (wl, k_part, n + off))]
    args = [a, w]
    if bias is not None:
        in_specs.append(pl.BlockSpec((1, 1, tn), lambda n, m: (wl, 0, n + off)))
        args.append(bias)
    if resid is not None:
        tiles_per_group = rows_per_group // tm
        in_specs += [pl.BlockSpec((tm, tn), lambda n, m: (m, n)),
                     pl.BlockSpec((1, gate.shape[1], tn), lambda n, m: (m // tiles_per_group, 0, n))]
        args += [resid, gate]
    hm_spec = pl.BlockSpec((tn // LANES, tm, LANES), lambda n, m: (n, m, 0))
    rows_spec = pl.BlockSpec((tm, tn), lambda n, m: (m, n))
    if head_major:
        assert ncols % tn == 0
    if head_major is True:
        out_shape, out_spec = jax.ShapeDtypeStruct((ncols // LANES, M, LANES), out_dtype), hm_spec
    elif head_major == "both":
        out_shape = [jax.ShapeDtypeStruct((M, ncols), out_dtype), jax.ShapeDtypeStruct((ncols // LANES, M, LANES), BF16)]
        out_spec = [rows_spec, hm_spec]
    else:
        out_shape, out_spec = jax.ShapeDtypeStruct((M, ncols), out_dtype), rows_spec
    osz = jnp.dtype(out_dtype).itemsize
    n_w = 2 if precise else 1
    vmem = (2 * tm * K * a.dtype.itemsize + 2 * K * tn * 4 + n_w * K * tn * 2 + 2 * tm * tn * osz
            + 4 * tm * tn * 4 + (2 * tm * tn * 4 if resid is not None else 0) + (K * tn * 4 if precise else 0))
    return pl.pallas_call(
        functools.partial(_mm_kernel, a_silu=a_silu, head_major=head_major, has_bias=bias is not None,
                          has_resid=resid is not None, precise=precise),
        grid=(n_tiles, M // tm), in_specs=in_specs, out_specs=out_spec, out_shape=out_shape,
        scratch_shapes=[pltpu.VMEM((K, tn), BF16)] * n_w,
        compiler_params=_params(("arbitrary", "arbitrary"), vmem), name="matmul")(*args)


def _swiglu_kernel(*refs, grouped, precise):
    if grouped:
        fresh_ref, rows_ref = refs[1], refs[2]
        refs = refs[4:]
    a_ref, w1_ref, w3_ref, o_ref, w1bf_ref, w3bf_ref = refs[:6]
    w1lo_ref, w3lo_ref = refs[6:] if precise else (None, None)
    m = pl.program_id(1)
    tm = a_ref.shape[0]
    fresh = (fresh_ref[m] == 1) if grouped else (m == 0)
    dot = functools.partial(jnp.dot, preferred_element_type=F32)

    @pl.when(fresh)
    def _():
        if precise:
            w1bf_ref[...], w1lo_ref[...] = _hi_lo(w1_ref[0])
            w3bf_ref[...], w3lo_ref[...] = _hi_lo(w3_ref[0])
        else:
            w1bf_ref[...] = w1_ref[0].astype(BF16)
            w3bf_ref[...] = w3_ref[0].astype(BF16)

    def compute(n_rows):
        if n_rows:
            a = a_ref[0:n_rows, :]
            if precise:
                a_hi, a_lo = _hi_lo(a)
                h1 = _dot3(a_hi, a_lo, w1bf_ref[...], w1lo_ref[...], dot)
                h3 = _dot3(a_hi, a_lo, w3bf_ref[...], w3lo_ref[...], dot)
            else:
                h1 = dot(a, w1bf_ref[...])
                h3 = dot(a, w3bf_ref[...])
            o_ref[0:n_rows, :] = (_silu(h1) * h3).astype(o_ref.dtype)
        if n_rows < tm:
            o_ref[n_rows:tm, :] = jnp.zeros((tm - n_rows, o_ref.shape[1]), o_ref.dtype)

    if grouped:
        rows = rows_ref[m]
        pl.when(rows > tm // 2)(functools.partial(compute, tm))
        pl.when((rows > 0) & (rows <= tm // 2))(functools.partial(compute, tm // 2))
        pl.when(rows == 0)(functools.partial(compute, 0))
    else:
        compute(tm)


def _swiglu_in(a, w1, w3, wl, *, tm, tn, tile_tables=None, precise=False):
    M, K = a.shape
    N = w1.shape[2]
    n_tiles = pl.cdiv(N, tn)
    grouped = tile_tables is not None
    if grouped:
        a_map = lambda n, m, te, tf, tr, tt: (tt[m], 0)
        w_map = lambda n, m, te, tf, tr, tt: (te[m], 0, n)
        o_map = lambda n, m, te, tf, tr, tt: (m, n)
    else:
        a_map = lambda n, m: (m, 0)
        w_map = lambda n, m: (wl, 0, n)
        o_map = lambda n, m: (m, n)
    in_specs = [pl.BlockSpec((tm, K), a_map), pl.BlockSpec((1, K, tn), w_map), pl.BlockSpec((1, K, tn), w_map)]
    out_spec = pl.BlockSpec((tm, tn), o_map)
    n_w = 4 if precise else 2
    scratch = [pltpu.VMEM((K, tn), BF16)] * n_w
    vmem = (2 * tm * K * a.dtype.itemsize + 4 * K * tn * 4 + n_w * K * tn * 2 + 2 * tm * tn * 4 + 6 * tm * tn * 4
            + (K * tn * 4 if precise else 0))
    kern = functools.partial(_swiglu_kernel, grouped=grouped, precise=precise)
    out_shape = jax.ShapeDtypeStruct((M, N), F32 if precise else BF16)
    cp = _params(("arbitrary", "arbitrary"), vmem)
    if grouped:
        gs = pltpu.PrefetchScalarGridSpec(num_scalar_prefetch=4, grid=(n_tiles, M // tm), in_specs=in_specs,
                                          out_specs=out_spec, scratch_shapes=scratch)
        return pl.pallas_call(kern, grid_spec=gs, out_shape=out_shape, compiler_params=cp,
                              name="moe_swiglu_in")(*tile_tables, a, w1, w3)
    return pl.pallas_call(kern, grid=(n_tiles, M // tm), in_specs=in_specs, out_specs=out_spec,
                          out_shape=out_shape, scratch_shapes=scratch, compiler_params=cp,
                          name="swiglu_in")(a, w1, w3)


def _expert_out_kernel(te_ref, fresh_ref, rows_ref, tt_ref, a_ref, w_ref, o_ref, *, sub):
    m, k = pl.program_id(0), pl.program_id(1)
    rows = rows_ref[m]
    tm = a_ref.shape[0]

    @pl.when(k == 0)
    def _():
        o_ref[...] = jnp.zeros_like(o_ref)

    def compute(n_rows):
        o_ref[0:n_rows, :] += jnp.dot(a_ref[0:n_rows, :], w_ref[0].astype(BF16), preferred_element_type=F32)

    for n_rows in range(sub, tm + 1, sub):
        pl.when((rows > n_rows - sub) & (rows <= n_rows))(functools.partial(compute, n_rows))


def _expert_out(h, w2, tile_tables, *, tm, tk, sub):
    S, F = h.shape
    N = w2.shape[2]
    n_k = F // tk
    k_eff = lambda m, k, tr: jnp.where(tr[m] > 0, k, n_k - 1)
    gs = pltpu.PrefetchScalarGridSpec(
        num_scalar_prefetch=4, grid=(S // tm, n_k),
        in_specs=[pl.BlockSpec((tm, tk), lambda m, k, te, tf, tr, tt: (tt[m], k_eff(m, k, tr))),
                  pl.BlockSpec((1, tk, N), lambda m, k, te, tf, tr, tt: (te[m], k_eff(m, k, tr), 0))],
        out_specs=pl.BlockSpec((tm, N), lambda m, k, te, tf, tr, tt: (m, 0)))
    vmem = 2 * tm * tk * 2 + 2 * tk * N * 4 + tk * N * 2 + 2 * tm * N * 4 + 2 * sub * N * 4
    return pl.pallas_call(functools.partial(_expert_out_kernel, sub=sub), grid_spec=gs,
                          out_shape=jax.ShapeDtypeStruct((S, N), F32),
                          compiler_params=_params(("arbitrary", "arbitrary"), vmem),
                          name="moe_expert_out")(*tile_tables, h, w2)


def _gdn_gates_kernel(ba_ref, alog_ref, dtb_ref, beta_ref, gc_ref, *, chunk, t_valid, n_heads):
    x = ba_ref[...]
    tc = x.shape[0]
    row = lax.broadcasted_iota(I32, (tc, tc), 0)
    col = lax.broadcasted_iota(I32, (tc, tc), 1)
    valid = (lax.broadcasted_iota(I32, x.shape, 0) % chunk) < t_valid
    beta_ref[...] = jnp.where(valid, 1.0 / (1.0 + jnp.exp(-x)), 0.0)
    y = x + dtb_ref[...]
    softplus = jnp.maximum(y, 0.0) + jnp.log1p(jnp.exp(-jnp.abs(y)))
    g = jnp.where(valid, -jnp.exp(alog_ref[...]) * softplus, 0.0)
    same_chunk_lower = ((row // chunk) == (col // chunk)) & (row >= col)
    gc_ref[...] = jnp.dot(jnp.where(same_chunk_lower, 1.0, 0.0), g, preferred_element_type=F32,
                          precision=HIGHEST)


def _gdn_gates(ba, a_log, dt_bias, *, t_valid, n_heads, tc):
    M = ba.shape[0]
    pad = lambda v: jnp.zeros((1, LANES), F32).at[0, n_heads:2 * n_heads].set(v.astype(F32))
    spec = pl.BlockSpec((tc, LANES), lambda i: (i, 0))
    vec = pl.BlockSpec((1, LANES), lambda i: (0, 0))
    return pl.pallas_call(
        functools.partial(_gdn_gates_kernel, chunk=GDN_CHUNK, t_valid=t_valid, n_heads=n_heads),
        grid=(M // tc,), in_specs=[spec, vec, vec], out_specs=[spec, spec],
        out_shape=[jax.ShapeDtypeStruct((M, LANES), F32)] * 2,
        compiler_params=_params(("arbitrary",), 0), name="gdn_gates")(ba, pad(a_log), pad(dt_bias))


def _gdn_kernel(q_ref, k_ref, v_ref, z_ref, cq_ref, ck_ref, cv_ref, wq_ref, wk_ref, wv_ref,
                bcol_ref, gcol_ref, grow_ref, s0_ref, onorm_ref, o_ref, sfin_ref,
                xq_ref, xk_ref, xv_ref, s_ref, *, hb, n_taps, dk, precise):
    c = pl.program_id(2)
    C = q_ref.shape[1]
    lead = xq_ref.shape[1] - C

    @pl.when(c == 0)
    def _():
        xq_ref[:, 0:lead, :] = cq_ref[:, 0]
        xk_ref[:, 0:lead, :] = ck_ref[:, 0]
        xv_ref[:, 0:lead, :] = cv_ref[:, 0]
        s_ref[...] = s0_ref[0]

    row = lax.broadcasted_iota(I32, (hb, C, C), 1)
    col = lax.broadcasted_iota(I32, (hb, C, C), 2)
    tri, stri = row >= col, row > col
    eye = jnp.where(row == col, 1.0, 0.0)
    def mxu(spec, a, b, three_pass):
        dot = lambda x, y: jnp.einsum(spec, x, y, preferred_element_type=F32)
        return _dot3(*_hi_lo(a), *_hi_lo(b), dot) if three_pass else dot(_bf(a), _bf(b))

    bmm = lambda a, b: mxu("hik,hkj->hij", a, b, precise)
    bmm3 = lambda a, b: mxu("hik,hkj->hij", a, b, True)

    def conv(x_ref, src_ref, w_ref):
        x_ref[:, lead:lead + C, :] = src_ref[...]
        first = lead - n_taps + 1
        y = x_ref[:, first:first + C, :] * w_ref[:, 0:1, :]
        for t in range(1, n_taps):
            y = y + x_ref[:, first + t:first + t + C, :] * w_ref[:, t:t + 1, :]
        x_ref[:, 0:lead, :] = x_ref[:, C:C + lead, :]
        return _silu(y)

    q = conv(xq_ref, q_ref, wq_ref)
    k = conv(xk_ref, k_ref, wk_ref)
    v = conv(xv_ref, v_ref, wv_ref)
    q = q * lax.rsqrt(jnp.sum(q * q, axis=-1, keepdims=True) + NORM_EPS) * (dk ** -0.5)
    k = k * lax.rsqrt(jnp.sum(k * k, axis=-1, keepdims=True) + NORM_EPS)
    bcol, gcol = bcol_ref[0], gcol_ref[0]
    beta = jnp.stack([bcol[:, j:j + 1] for j in range(hb)])
    gc = jnp.stack([gcol[:, j:j + 1] for j in range(hb)])
    gr = grow_ref[0, 0][:, None, :]
    decay = jnp.where(tri, jnp.exp(jnp.where(tri, gc - gr, 0.0)), 0.0)
    kb = k * beta
    eg = jnp.exp(gc)
    kq = mxu("hid,hjd->hij", jnp.concatenate([kb, q], axis=1), k, precise)
    a_mat = jnp.where(stri, kq[:, :C] * decay, 0.0)
    attn = jnp.where(tri, kq[:, C:] * decay, 0.0)
    inv = eye - a_mat
    pw = a_mat
    for _ in range(max(C.bit_length() - 2, 0)):
        pw = bmm3(pw, pw)
        inv = inv + bmm3(inv, pw)
    dv = v.shape[2]
    sol = bmm(inv, jnp.concatenate([v * beta, kb * eg], axis=2))
    u, w = sol[:, :, :dv], sol[:, :, dv:]
    s = s_ref[...]
    ws_qs = bmm(jnp.concatenate([w, q * eg], axis=1), s)
    v_new = u - ws_qs[:, :C]
    o = ws_qs[:, C:] + bmm(attn, v_new)
    g_last = gc[:, C - 1:C, :]
    k_dec = k * jnp.exp(g_last - gc)
    if precise:
        (k_hi, k_lo), (v_hi, v_lo) = _hi_lo(k_dec), _hi_lo(v_new)
        kv = [_dot3(k_hi[j], k_lo[j], v_hi[j], v_lo[j], _dot_tn) for j in range(hb)]
    else:
        k_bf, v_bf = _bf(k_dec), _bf(v_new)
        kv = [_dot_tn(k_bf[j], v_bf[j]) for j in range(hb)]
    s_new = s * jnp.exp(g_last) + jnp.stack(kv)
    s_ref[...] = s_new
    sfin_ref[0] = s_new
    o = o * lax.rsqrt(jnp.mean(o * o, axis=-1, keepdims=True) + NORM_EPS) * onorm_ref[...]
    o = (o * _silu(z_ref[...])).astype(o_ref.dtype)
    for j in range(hb):
        o_ref[:, j * LANES:(j + 1) * LANES] = o[j]


def _gdn_core(proj_hm, ba, conv_buf, s0, conv_w, a_log, dt_bias, o_norm, *, batch, n_chunks, t_valid, hb,
              precise):
    H = s0.shape[1]
    dk = s0.shape[2]
    C = GDN_CHUNK
    M = proj_hm.shape[1]
    n_taps = conv_w.shape[0]
    lead = 8
    nhb = H // hb
    beta, gc = _gdn_gates(ba, a_log, dt_bias, t_valid=t_valid, n_heads=H, tc=min(M, 256))
    beta_col = beta[:, :H].reshape(M, nhb, hb).transpose(1, 0, 2)
    gc_col = gc[:, H:2 * H].reshape(M, nhb, hb).transpose(1, 0, 2)
    gc_row = gc[:, H:2 * H].reshape(M // C, C, nhb, hb).transpose(2, 0, 3, 1)
    cb = conv_buf.astype(F32).reshape(batch, n_taps - 1, 3 * H, LANES).transpose(2, 0, 1, 3)
    cb = jnp.pad(cb, ((0, 0), (0, 0), (lead - (n_taps - 1), 0), (0, 0)))
    cw = jnp.pad(conv_w.astype(F32).reshape(n_taps, 3 * H, LANES).transpose(1, 0, 2),
                 ((0, 0), (0, 8 - n_taps), (0, 0)))

    def tile_spec(section):
        return pl.BlockSpec((hb, C, LANES), lambda b, h, c: (section * nhb + h, b * n_chunks + c, 0))

    def cb_spec(section):
        return pl.BlockSpec((hb, 1, lead, LANES), lambda b, h, c: (section * nhb + h, b, 0, 0))

    def cw_spec(section):
        return pl.BlockSpec((hb, 8, LANES), lambda b, h, c: (section * nhb + h, 0, 0))

    col_spec = pl.BlockSpec((1, C, hb), lambda b, h, c: (h, b * n_chunks + c, 0))
    in_specs = [tile_spec(0), tile_spec(1), tile_spec(2), tile_spec(3), cb_spec(0), cb_spec(1), cb_spec(2),
                cw_spec(0), cw_spec(1), cw_spec(2), col_spec, col_spec,
                pl.BlockSpec((1, 1, hb, C), lambda b, h, c: (h, b * n_chunks + c, 0, 0)),
                pl.BlockSpec((1, hb, dk, LANES), lambda b, h, c: (b, h, 0, 0)),
                pl.BlockSpec((1, LANES), lambda b, h, c: (0, 0))]
    out_specs = [pl.BlockSpec((C, hb * LANES), lambda b, h, c: (b * n_chunks + c, h)),
                 pl.BlockSpec((1, hb, dk, LANES), lambda b, h, c: (b, h, 0, 0))]
    out_shape = [jax.ShapeDtypeStruct((M, H * LANES), F32 if precise else BF16),
                 jax.ShapeDtypeStruct(s0.shape, F32)]
    scratch = [pltpu.VMEM((hb, C + lead, LANES), F32)] * 3 + [pltpu.VMEM((hb, dk, LANES), F32)]
    return pl.pallas_call(
        functools.partial(_gdn_kernel, hb=hb, n_taps=n_taps, dk=dk, precise=precise),
        grid=(batch, nhb, n_chunks), in_specs=in_specs, out_specs=out_specs, out_shape=out_shape,
        scratch_shapes=scratch, compiler_params=_params(("arbitrary",) * 3, 0), name="gdn_delta")(
            proj_hm, proj_hm, proj_hm, proj_hm, cb, cb, cb, cw, cw, cw, beta_col, gc_col, gc_row,
            s0.astype(F32), o_norm.reshape(1, LANES).astype(F32))


def _gdn_mixer(h, x, gate, conv_buf, s0, w_in, w_ba, conv_w, a_log, dt_bias, o_norm, w_out, *,
               batch, seq, rows_per_group, tm, precise):
    H = s0.shape[1]
    n_taps = conv_w.shape[0]
    C = GDN_CHUNK
    M = batch * seq
    qkvz = 4 * H * LANES
    proj_hm = _matmul(h, w_in, 0, ncols=qkvz, tm=tm, tn=512, head_major=True, precise=precise)
    ba = _matmul(h, w_ba, 0, tm=tm, tn=LANES, precise=precise)
    keep = min(seq, n_taps - 1)
    last_rows = proj_hm.reshape(4 * H, batch, seq, LANES)[:3 * H, :, seq - keep:]
    if seq % C == 0:
        n_chunks, t_valid, padded = seq // C, C, proj_hm
        ba_p = ba
    else:
        n_chunks, t_valid = 1, seq
        padded = jnp.pad(proj_hm.reshape(4 * H, batch, seq, LANES), ((0, 0), (0, 0), (0, C - seq), (0, 0)))
        padded = padded.reshape(4 * H, batch * C, LANES)
        ba_p = jnp.pad(ba.reshape(batch, seq, LANES), ((0, 0), (0, C - seq), (0, 0))).reshape(batch * C, LANES)
    o_rows, s_fin = _gdn_core(padded, ba_p, conv_buf, s0, conv_w, a_log, dt_bias, o_norm, batch=batch,
                            n_chunks=n_chunks, t_valid=t_valid, hb=min(16, H), precise=precise)
    o_rows = o_rows.reshape(batch, n_chunks * C, H * LANES)[:, :seq].reshape(M, H * LANES)
    x_new = _matmul(o_rows, w_out, 0, tm=tm, tn=512, resid=x, gate=gate, rows_per_group=rows_per_group,
                    precise=precise)
    new_buf = last_rows.transpose(1, 2, 0, 3).reshape(batch, keep, 3 * H * LANES)
    if keep < n_taps - 1:
        new_buf = jnp.concatenate([conv_buf.astype(F32)[:, keep:], new_buf], axis=1)
    return x_new, new_buf, s_fin


def _ordered_key(score):
    bits = pltpu.bitcast(score + 0.0, I32)
    return bits ^ ((bits >> 31) & 0x7FFFFFFF)


def _topk_select(keys, col, k):
    count = lambda m: jnp.sum(jnp.sum(m.astype(I32), axis=2, keepdims=True), axis=0, keepdims=True)
    n_rows = keys.shape[1]
    n_cols = keys.shape[0] * keys.shape[2]
    zero = jnp.zeros((1, n_rows, 1), I32)
    thr = jnp.where(count(keys >= zero) >= k, zero, jnp.full((1, n_rows, 1), INT32_MIN, I32))

    def bit_step(i, thr):
        cand = thr | (1 << (30 - i))
        return jnp.where(count(keys >= cand) >= k, cand, thr)

    thr = lax.fori_loop(0, 31, bit_step, thr)
    above = keys > thr
    tied = keys == thr
    need = k - count(above)
    n_bits = n_cols.bit_length()

    def tie_step(i, bound):
        cand = bound + (1 << (n_bits - 1 - i))
        ok = (cand <= n_cols) & (count(tied & (col < cand)) <= need)
        return jnp.where(ok, cand, bound)

    bound = lax.fori_loop(0, n_bits, tie_step, zero)
    return above | (tied & (col < bound))


def _dsa_prompt_kernel(qi_ref, wi_ref, ki_ref, q_ref, k_ref, v_ref, o_ref, sc_ref, bias_ref,
                       m_ref, l_ref, acc_ref, ohm_ref, *, hc, idx_scale, att_scale, topk, group):
    i = pl.program_id(1)
    n_kt, n_q, kt = sc_ref.shape
    n_chunks = qi_ref.shape[0] // hc
    n_live = (i * n_q + n_q - 1) // kt + 1
    sc_ref[...] = jnp.zeros_like(sc_ref)

    def score_tile(t, carry):
        ki = ki_ref[0, pl.ds(pl.multiple_of(t * kt, kt), kt), :]

        def chunk(c, carry):
            lhs = qi_ref[pl.ds(c * hc, hc)].reshape(hc * n_q, qi_ref.shape[2])
            d = jnp.maximum(_dot_nt(lhs, ki), 0.0).reshape(hc, n_q, kt)
            w = wi_ref[c]
            acc = d[0] * w[:, 0:1]
            for j in range(1, hc):
                acc = acc + d[j] * w[:, j:j + 1]
            sc_ref[t] += acc
            return carry

        return lax.fori_loop(0, n_chunks, chunk, carry)

    lax.fori_loop(0, n_live, score_tile, 0)
    shape = sc_ref.shape
    col = lax.broadcasted_iota(I32, shape, 0) * kt + lax.broadcasted_iota(I32, shape, 2)
    allowed = col <= i * n_q + lax.broadcasted_iota(I32, shape, 1)
    keys = jnp.where(allowed, _ordered_key(sc_ref[...] * idx_scale), INT32_MIN)
    sel = _topk_select(keys, col, topk) & allowed
    bias_ref[...] = jnp.where(sel, 0.0, -jnp.inf)

    def grp(g, carry):
        qg = q_ref[pl.ds(g * group, group)].reshape(group * n_q, q_ref.shape[2])
        m_ref[...] = jnp.full_like(m_ref, -jnp.inf)
        l_ref[...] = jnp.zeros_like(l_ref)
        acc_ref[...] = jnp.zeros_like(acc_ref)

        def key_tile(t, carry):
            rows = pl.ds(pl.multiple_of(t * kt, kt), kt)
            s = _dot_nt(qg, k_ref[g, rows, :]) * att_scale
            s = s.reshape(group, n_q, kt) + bias_ref[t][None]
            m_old = m_ref[...]
            m_new = jnp.maximum(m_old, jnp.max(s, axis=-1, keepdims=True))
            m_safe = jnp.where(m_new == -jnp.inf, 0.0, m_new)
            alpha = jnp.exp(m_old - m_safe)
            p = jnp.exp(s - m_safe)
            l_ref[...] = alpha * l_ref[...] + jnp.sum(p, axis=-1, keepdims=True)
            pv = jnp.dot(p.reshape(group * n_q, kt).astype(BF16), v_ref[g, rows, :],
                         preferred_element_type=F32)
            acc_ref[...] = alpha * acc_ref[...] + pv.reshape(acc_ref.shape)
            m_ref[...] = m_new
            return carry

        lax.fori_loop(0, n_live, key_tile, 0)
        ohm_ref[pl.ds(g * group, group)] = (acc_ref[...] / l_ref[...]).astype(ohm_ref.dtype)
        return carry

    lax.fori_loop(0, k_ref.shape[0], grp, 0)
    dh = ohm_ref.shape[2]
    for h in range(ohm_ref.shape[0]):
        o_ref[:, h * dh:(h + 1) * dh] = ohm_ref[h]


def _dsa_prompt(q_hm, qi_hm, wi, ki, kv_hm, *, batch, seq, topk):
    HQ, M, dh = q_hm.shape
    HI = qi_hm.shape[0]
    G = kv_hm.shape[0] // 2
    nq = ATT_BLOCK
    nb = seq // nq
    hc = 4
    kt = 512 if seq % 512 == 0 else seq
    wi_c = wi.reshape(M, HI // hc, hc).transpose(1, 0, 2)
    di = qi_hm.shape[2]
    group = HQ // G
    kern = functools.partial(_dsa_prompt_kernel, hc=hc, idx_scale=float(di ** -0.5 * HI ** -0.5),
                             att_scale=float(dh ** -0.5), topk=topk, group=group)
    vmem = (2 * HI * nq * di * 2 + 5 * HQ * nq * dh * 2 + 4 * G * seq * dh * 2 + 2 * seq * di * 2
            + 8 * nq * seq * 4 + 3 * hc * nq * kt * 4 + 5 * group * nq * kt * 4 + (4 << 20))
    return pl.pallas_call(
        kern, grid=(batch, nb),
        in_specs=[pl.BlockSpec((HI, nq, di), lambda b, i: (0, b * nb + i, 0)),
                  pl.BlockSpec((HI // hc, nq, hc), lambda b, i: (0, b * nb + i, 0)),
                  pl.BlockSpec((1, seq, di), lambda b, i: (b, 0, 0)),
                  pl.BlockSpec((HQ, nq, dh), lambda b, i: (0, b * nb + i, 0)),
                  pl.BlockSpec((G, seq, dh), lambda b, i: (0, b, 0)),
                  pl.BlockSpec((G, seq, dh), lambda b, i: (1, b, 0))],
        out_specs=pl.BlockSpec((nq, HQ * dh), lambda b, i: (b * nb + i, 0)),
        out_shape=jax.ShapeDtypeStruct((M, HQ * dh), BF16),
        scratch_shapes=[pltpu.VMEM((seq // kt, nq, kt), F32), pltpu.VMEM((seq // kt, nq, kt), F32),
                        pltpu.VMEM((group, nq, 1), F32), pltpu.VMEM((group, nq, 1), F32),
                        pltpu.VMEM((group, nq, dh), F32), pltpu.VMEM((HQ, nq, dh), BF16)],
        compiler_params=_params(("arbitrary", "arbitrary"), vmem), name="dsa_prompt")(
            qi_hm, wi_c, ki, q_hm, kv_hm, kv_hm)


def _dsa_sample_scores_kernel(pt_ref, qi_ref, wi_ref, pool_ref, new_ref, o_ref, *, n_pages, n_tok, idx_scale):
    p = pl.program_id(1)
    page = jnp.where(p < n_pages, pool_ref[0], new_ref[0]).astype(BF16)
    d = jnp.maximum(_dot_nt(qi_ref[0], page), 0.0) * wi_ref[0]
    hi = d.shape[0] // n_tok
    o_ref[0] = jnp.sum(d.reshape(n_tok, hi, d.shape[1]), axis=1) * idx_scale


def _dsa_sample_select_kernel(sc_ref, o_ref, *, n_past, topk):
    sc = sc_ref[0]
    col = lax.broadcasted_iota(I32, sc.shape, 1)
    q_pos = n_past + lax.broadcasted_iota(I32, sc.shape, 0)
    allowed = col <= q_pos
    keys = jnp.where(allowed, _ordered_key(sc), INT32_MIN)
    sel = _topk_select(keys[None], col[None], topk)[0] & allowed
    o_ref[0] = jnp.where(sel, 0.0, -jnp.inf)


def _dsa_sample_attn_kernel(pt_ref, q_ref, bias_ref, hmask_ref, kp_ref, vp_ref, kn_ref, vn_ref, o_ref,
                            m_ref, l_ref, acc_ref, *, n_pages, att_scale):
    p = pl.program_id(1)
    n_tok = bias_ref.shape[1]
    heads = q_ref.shape[1] // n_tok

    @pl.when(p == 0)
    def _():
        m_ref[...] = jnp.full_like(m_ref, -jnp.inf)
        l_ref[...] = jnp.zeros_like(l_ref)
        acc_ref[...] = jnp.zeros_like(acc_ref)

    def step(k_ref, v_ref):
        cols = k_ref.shape[1] * k_ref.shape[2]
        k_hi, k_lo = _hi_lo(k_ref[0].reshape(cols, k_ref.shape[3]))
        v_hi, v_lo = _hi_lo(v_ref[0].reshape(cols, v_ref.shape[3]))
        bias = jnp.concatenate([jnp.broadcast_to(bias_ref[0, t:t + 1, :], (heads, cols)) for t in range(n_tok)],
                               axis=0)
        s = _dot3(*_hi_lo(q_ref[0]), k_hi, k_lo, _dot_nt) * att_scale + (bias + hmask_ref[...])
        m_old = m_ref[...]
        m_new = jnp.maximum(m_old, jnp.max(s, axis=-1, keepdims=True))
        m_safe = jnp.where(m_new == -jnp.inf, 0.0, m_new)
        alpha = jnp.exp(m_old - m_safe)
        pr = jnp.exp(s - m_safe)
        l_ref[...] = alpha * l_ref[...] + jnp.sum(pr, axis=-1, keepdims=True)
        pv = _dot3(*_hi_lo(pr), v_hi, v_lo, functools.partial(jnp.dot, preferred_element_type=F32))
        acc_ref[...] = alpha * acc_ref[...] + pv
        m_ref[...] = m_new

    pl.when(p < n_pages)(functools.partial(step, kp_ref, vp_ref))
    pl.when(p == n_pages)(functools.partial(step, kn_ref, vn_ref))

    @pl.when(p == pl.num_programs(1) - 1)
    def _():
        o_ref[0] = (acc_ref[...] / l_ref[...]).astype(o_ref.dtype)


def _dsa_sample(q, qi, wi, ki_new, k_new, v_new, k_pool, v_pool, ki_pool, page_table, *, topk):
    B, T, _ = q.shape
    n_pool, page, G, dh = k_pool.shape
    di = ki_pool.shape[2]
    HI = wi.shape[2]
    HQ = q.shape[2] // dh
    n_rep = HQ // G
    P = page_table.shape[1]
    n_past = P * page
    s_pad = (P + 1) * page
    idx_scale = float(di ** -0.5 * HI ** -0.5)
    pad_rows = lambda a: jnp.pad(a, ((0, 0), (0, page - T)) + ((0, 0),) * (a.ndim - 2))
    last = lambda b, p, pt: pt[b, jnp.minimum(p, P - 1)]

    scores = pl.pallas_call(
        functools.partial(_dsa_sample_scores_kernel, n_pages=P, n_tok=T, idx_scale=idx_scale),
        grid_spec=pltpu.PrefetchScalarGridSpec(
            num_scalar_prefetch=1, grid=(B, P + 1),
            in_specs=[pl.BlockSpec((1, T * HI, di), lambda b, p, pt: (b, 0, 0)),
                      pl.BlockSpec((1, T * HI, 1), lambda b, p, pt: (b, 0, 0)),
                      pl.BlockSpec((1, page, di), lambda b, p, pt: (last(b, p, pt), 0, 0)),
                      pl.BlockSpec((1, page, di), lambda b, p, pt: (b, 0, 0))],
            out_specs=pl.BlockSpec((1, T, page), lambda b, p, pt: (b, 0, p))),
        out_shape=jax.ShapeDtypeStruct((B, T, s_pad), F32),
        compiler_params=_params(("arbitrary", "arbitrary"), 0), name="dsa_sample_scores")(
            page_table, qi.reshape(B, T * HI, di), wi.reshape(B, T * HI, 1), ki_pool, pad_rows(ki_new))

    bias = pl.pallas_call(
        functools.partial(_dsa_sample_select_kernel, n_past=n_past, topk=topk),
        grid=(B,), in_specs=[pl.BlockSpec((1, T, s_pad), lambda b: (b, 0, 0))],
        out_specs=pl.BlockSpec((1, T, s_pad), lambda b: (b, 0, 0)),
        out_shape=jax.ShapeDtypeStruct((B, T, s_pad), F32),
        compiler_params=_params(("arbitrary",), 0), name="dsa_sample_select")(scores)

    rows, cols = T * HQ, page * G
    row_g = (jnp.arange(rows, dtype=I32) % HQ) // n_rep
    hmask = jnp.where(row_g[:, None] == jnp.arange(cols, dtype=I32)[None, :] % G, 0.0, -jnp.inf).astype(F32)
    o = pl.pallas_call(
        functools.partial(_dsa_sample_attn_kernel, n_pages=P, att_scale=float(dh ** -0.5)),
        grid_spec=pltpu.PrefetchScalarGridSpec(
            num_scalar_prefetch=1, grid=(B, P + 1),
            in_specs=[pl.BlockSpec((1, rows, dh), lambda b, p, pt: (b, 0, 0)),
                      pl.BlockSpec((1, T, cols), lambda b, p, pt: (b, 0, p)),
                      pl.BlockSpec((rows, cols), lambda b, p, pt: (0, 0)),
                      pl.BlockSpec((1, page, G, dh), lambda b, p, pt: (last(b, p, pt), 0, 0, 0)),
                      pl.BlockSpec((1, page, G, dh), lambda b, p, pt: (last(b, p, pt), 0, 0, 0)),
                      pl.BlockSpec((1, page, G, dh), lambda b, p, pt: (b, 0, 0, 0)),
                      pl.BlockSpec((1, page, G, dh), lambda b, p, pt: (b, 0, 0, 0))],
            out_specs=pl.BlockSpec((1, rows, dh), lambda b, p, pt: (b, 0, 0)),
            scratch_shapes=[pltpu.VMEM((rows, 1), F32), pltpu.VMEM((rows, 1), F32),
                            pltpu.VMEM((rows, dh), F32)]),
        out_shape=jax.ShapeDtypeStruct((B, rows, dh), F32),
        compiler_params=_params(("arbitrary", "arbitrary"), 0), name="dsa_sample_attn")(
            page_table, q.reshape(B, rows, dh), jnp.repeat(bias, G, axis=2), hmask, k_pool, v_pool,
            pad_rows(k_new), pad_rows(v_new))
    return o.reshape(B, T, HQ * dh)


def _gather_rows_kernel(idx_ref, rows_ref, x_hbm, o_ref, buf_ref, sem, *, tm):
    i = pl.program_id(0)
    base = i * tm
    n_rows = rows_ref[i]

    def copy(r):
        return pltpu.make_async_copy(x_hbm.at[pl.ds(idx_ref[base + r], 1)], buf_ref.at[pl.ds(r, 1)], sem)

    def start(r, c):
        copy(r).start()
        return c

    def wait(r, c):
        copy(r).wait()
        return c

    buf_ref[...] = jnp.zeros_like(buf_ref)
    lax.fori_loop(0, n_rows, start, 0)
    lax.fori_loop(0, n_rows, wait, 0)
    o_ref[...] = buf_ref[...].astype(o_ref.dtype)


def _gather_rows(x, idx, rows, *, tm, out_dtype):
    S = idx.shape[0]
    W = x.shape[1]
    return pl.pallas_call(
        functools.partial(_gather_rows_kernel, tm=tm),
        grid_spec=pltpu.PrefetchScalarGridSpec(
            num_scalar_prefetch=2, grid=(S // tm,),
            in_specs=[pl.BlockSpec(memory_space=pl.ANY)],
            out_specs=pl.BlockSpec((tm, W), lambda i, idx, rows: (i, 0)),
            scratch_shapes=[pltpu.VMEM((tm, W), x.dtype), pltpu.SemaphoreType.DMA]),
        out_shape=jax.ShapeDtypeStruct((S, W), out_dtype),
        compiler_params=_params(("arbitrary",), 6 * tm * W * 4), name="moe_gather")(idx, rows, x)


def _combine_kernel(slot_ref, y_hbm, x_ref, gt_ref, rg_ref, fg_ref, o_ref, buf_ref, sem, *, tm, n_top, row0):
    base = row0 + pl.program_id(0) * tm

    def copy(r, j):
        return pltpu.make_async_copy(y_hbm.at[pl.ds(slot_ref[(base + r) * n_top + j], 1)],
                                     buf_ref.at[j, pl.ds(r, 1)], sem)

    def start(r, c):
        for j in range(n_top):
            copy(r, j).start()
        return c

    def wait(r, c):
        for j in range(n_top):
            copy(r, j).wait()
        return c

    lax.fori_loop(0, tm, start, 0)
    lax.fori_loop(0, tm, wait, 0)
    rg = rg_ref[...]
    moe = buf_ref[0] * rg[:, 0:1]
    for j in range(1, n_top):
        moe = moe + buf_ref[j] * rg[:, j:j + 1]
    x = x_ref[...] + gt_ref[0] * moe
    o_ref[...] = x * lax.rsqrt(jnp.mean(x * x, axis=-1, keepdims=True) + NORM_EPS) * fg_ref[...]


def _combine(y_slots, slot_of, x, gate, route_gate, final_g, *, rows_per_group, tm, row0):
    M, D = x.shape
    tiles_per_group = rows_per_group // tm
    t0 = row0 // tm
    return pl.pallas_call(
        functools.partial(_combine_kernel, tm=tm, n_top=MOE_TOPK, row0=row0),
        grid_spec=pltpu.PrefetchScalarGridSpec(
            num_scalar_prefetch=1, grid=(M // tm,),
            in_specs=[pl.BlockSpec(memory_space=pl.ANY),
                      pl.BlockSpec((tm, D), lambda i, s: (i, 0)),
                      pl.BlockSpec((1, gate.shape[1], D), lambda i, s: (i // tiles_per_group, 0, 0)),
                      pl.BlockSpec((tm, LANES), lambda i, s: (i + t0, 0)),
                      pl.BlockSpec((1, D), lambda i, s: (0, 0))],
            out_specs=pl.BlockSpec((tm, D), lambda i, s: (i, 0)),
            scratch_shapes=[pltpu.VMEM((MOE_TOPK, tm, D), F32), pltpu.SemaphoreType.DMA]),
        out_shape=jax.ShapeDtypeStruct((M, D), F32),
        compiler_params=_params(("arbitrary",), (MOE_TOPK + 8) * tm * D * 4), name="moe_combine")(
            slot_of, y_slots, x, gate, route_gate, final_g.reshape(1, D))


def _tile_tables(rows, expert):
    t = jnp.arange(rows.shape[0], dtype=I32)
    seen = (t[None, :] <= t[:, None]) & (rows[None, :] > 0)
    tile = jnp.max(jnp.where(seen, t[None, :], 0), axis=1)
    expert = expert[tile]
    fresh = jnp.concatenate([jnp.ones((1,), I32), (expert[1:] != expert[:-1]).astype(I32)])
    return expert.astype(I32), fresh, rows.astype(I32), tile.astype(I32)


def _route_tables(expert_idx, n_experts, align, tm):
    N, top = expert_idx.shape
    flat = expert_idx.reshape(-1)
    ids = jnp.arange(n_experts, dtype=I32)
    onehot = (flat[:, None] == ids[None, :]).astype(I32)
    rank = jnp.sum((jnp.cumsum(onehot, axis=0) - 1) * onehot, axis=1)
    counts = jnp.sum(onehot, axis=0)
    blocks = (counts + align - 1) // align
    blk_end = jnp.cumsum(blocks)
    blk_start = blk_end - blocks
    n_blocks = (N * top + n_experts * (align - 1)) // align
    slot_of = (jnp.sum(onehot * blk_start[None, :], axis=1) * align + rank).astype(I32)
    src_row = jnp.zeros((n_blocks * align,), I32).at[slot_of].set(jnp.arange(N * top, dtype=I32) // top)
    b = jnp.arange(n_blocks, dtype=I32)
    blk_expert = jnp.minimum(jnp.sum((blk_end[None, :] <= b[:, None]).astype(I32), axis=1), n_experts - 1)
    blk_rows = jnp.clip(counts[blk_expert] - (b - blk_start[blk_expert]) * align, 0, align)
    blk_rows = jnp.where(b < blk_end[-1], blk_rows, 0)
    per = align // tm
    tile_rows = jnp.clip(jnp.repeat(blk_rows, per) - jnp.tile(jnp.arange(per, dtype=I32) * tm, n_blocks), 0, tm)
    return (src_row, slot_of, _tile_tables(tile_rows, jnp.repeat(blk_expert, per)),
            _tile_tables(blk_rows, blk_expert))


MOE_ALIGN = 1024
MOE_TILE = 512
MOE_SUB = 256


def _moe(h_rows, expert_idx, w1, w3, w2):
    src_row, slot_of, tables, blk_tables = _route_tables(expert_idx, w1.shape[0], MOE_ALIGN, MOE_TILE)
    xg = _gather_rows(h_rows, src_row, tables[2], tm=MOE_TILE, out_dtype=BF16)
    hmid = _swiglu_in(xg, w1, w3, None, tm=MOE_TILE, tn=512, tile_tables=tables)
    y = _expert_out(hmid, w2, blk_tables, tm=MOE_ALIGN, tk=256, sub=MOE_SUB)
    return y, slot_of


def _pad_cols(w, width):
    return jnp.pad(w, ((0, 0), (0, 0), (0, width - w.shape[2])))


def kernel(x_prompt, x_sample, state_gdn, state_conv, cache_k, cache_v, cache_kidx, page_table, c_prompt, c_sample, ada_w, ada_b, norm1_g, norm2_g, gdn_w_in, gdn_conv_w, gdn_a_log, gdn_dt_bias, gdn_o_norm, gdn_w_out, att_w_in, att_w_out, ffn_w1, ffn_w3, ffn_w2, moe_router_w, moe_w1, moe_w3, moe_w2, final_norm_g):
    BP, TP, D = x_prompt.shape
    BS, TS, _ = x_sample.shape
    MP, MS = BP * TP, BS * TS
    H = state_gdn.shape[2]
    G, dh = cache_k.shape[3], cache_k.shape[4]
    di = cache_kidx.shape[3]
    HQ = att_w_out.shape[1] // dh
    q_w, kv_w = HQ * dh, G * dh
    HI = (att_w_in.shape[2] - q_w - 2 * kv_w - di) // (di + 1)
    qi_w = HI * di
    n_experts = moe_w1.shape[1]
    tmp = 512 if MP % 512 == 0 else MP
    tms = MS

    n_c = BP + BS
    c_all = jnp.pad(jnp.concatenate([c_prompt, c_sample], axis=0), ((0, -n_c % 16), (0, 0)))
    mods = [_matmul(c_all, ada_w, l, tm=c_all.shape[0], tn=512, a_silu=True, bias=ada_b.reshape(-1, 1, 6 * D),
                    precise=True) for l in range(ada_w.shape[0])]

    def mod_p(l, j):
        return mods[l][:BP, j * D:(j + 1) * D].reshape(BP, 1, D)

    def mod_s(l, j):
        return jnp.repeat(mods[l][BP:n_c, j * D:(j + 1) * D], TS, axis=0).reshape(1, MS, D)

    xp = x_prompt.reshape(MP, D)
    xs = x_sample.reshape(MS, D)

    hp = _modulate(xp, norm1_g[0], mod_p(0, 0), mod_p(0, 1), rows_per_group=TP, tm=256, out_dtype=BF16)
    hs = _modulate(xs, norm1_g[0], mod_s(0, 0), mod_s(0, 1), rows_per_group=MS, tm=tms, out_dtype=F32)
    qkvz = 4 * H * LANES
    w_ba = _pad_cols(gdn_w_in[:, :, qkvz:], LANES)
    zeros_buf = jnp.zeros((BP,) + state_conv.shape[2:], F32)
    zeros_s = jnp.zeros((BP,) + state_gdn.shape[2:], F32)
    gdn = functools.partial(_gdn_mixer, w_in=gdn_w_in, w_ba=w_ba, conv_w=gdn_conv_w[0], a_log=gdn_a_log[0],
                            dt_bias=gdn_dt_bias[0], o_norm=gdn_o_norm[0], w_out=gdn_w_out)
    xp, conv_p, s_p = gdn(hp, xp, mod_p(0, 2), zeros_buf, zeros_s, batch=BP, seq=TP, rows_per_group=TP, tm=tmp,
                          precise=False)
    xs, conv_s, s_s = gdn(hs, xs, mod_s(0, 2), state_conv[0], state_gdn[0], batch=BS, seq=TS,
                          rows_per_group=MS, tm=tms, precise=True)

    def ffn(x, shift, scale, gate, rows_per_group, tm, tmod, precise):
        tn = 256 if precise else 512
        h = _modulate(x, norm2_g[0], shift, scale, rows_per_group=rows_per_group, tm=tmod,
                      out_dtype=F32 if precise else BF16)
        mid = _swiglu_in(h, ffn_w1, ffn_w3, 0, tm=tm, tn=tn, precise=precise)
        k_parts = 2 if mid.shape[1] % (2 * LANES) == 0 else 1
        for part in range(k_parts):
            x = _matmul(mid, ffn_w2, 0, tm=tm, tn=tn, resid=x, gate=gate, rows_per_group=rows_per_group,
                        k_parts=k_parts, k_part=part, precise=precise)
        return x

    xp = ffn(xp, mod_p(0, 3), mod_p(0, 4), mod_p(0, 5), TP, tmp, 256, False)
    xs = ffn(xs, mod_s(0, 3), mod_s(0, 4), mod_s(0, 5), MS, tms, tms, True)

    hp = _modulate(xp, norm1_g[1], mod_p(1, 0), mod_p(1, 1), rows_per_group=TP, tm=256, out_dtype=BF16)
    hs = _modulate(xs, norm1_g[1], mod_s(1, 0), mod_s(1, 1), rows_per_group=MS, tm=tms, out_dtype=F32)
    w_kiwi = _pad_cols(att_w_in[:, :, q_w + 2 * kv_w + qi_w:], 2 * LANES)

    def att_proj(h, tm, head_major, precise):
        q = _matmul(h, att_w_in, 0, col0=0, ncols=q_w, tm=tm, tn=512, out_dtype=F32 if precise else BF16,
                    head_major=head_major, precise=precise)
        kv = _matmul(h, att_w_in, 0, col0=q_w, ncols=2 * kv_w, tm=tm, tn=512,
                     head_major="both" if head_major else False, precise=precise)
        qi = _matmul(h, att_w_in, 0, col0=q_w + 2 * kv_w, ncols=qi_w, tm=tm, tn=512, out_dtype=BF16,
                     head_major=head_major, precise=precise)
        kiwi = _matmul(h, w_kiwi, 0, tm=tm, tn=2 * LANES, precise=precise)
        return q, kv, qi, kiwi[:, :di], kiwi[:, di:di + HI]

    q, (kv_p, kv_hm), qi, ki_p, wi = att_proj(hp, tmp, True, False)
    k_p, v_p = kv_p[:, :kv_w], kv_p[:, kv_w:]
    o_rows = _dsa_prompt(q, qi, wi, ki_p.astype(BF16).reshape(BP, TP, di), kv_hm, batch=BP, seq=TP,
                         topk=min(IDX_TOPK, TP // 4))
    xp = _matmul(o_rows, att_w_out, 0, tm=tmp, tn=512, resid=xp, gate=mod_p(1, 2), rows_per_group=TP)

    q, kv_s, qi, ki_s, wi = att_proj(hs, tms, False, True)
    k_s, v_s = kv_s[:, :kv_w], kv_s[:, kv_w:]
    n_past = page_table.shape[1] * cache_k.shape[2]
    o_s = _dsa_sample(q.reshape(BS, TS, q_w), qi.reshape(BS, TS, qi_w), wi.reshape(BS, TS, HI),
                      ki_s.reshape(BS, TS, di), k_s.reshape(BS, TS, G, dh), v_s.reshape(BS, TS, G, dh),
                      cache_k.reshape(cache_k.shape[1:]), cache_v.reshape(cache_v.shape[1:]),
                      cache_kidx.reshape(cache_kidx.shape[1:]), page_table,
                      topk=min(IDX_TOPK, (n_past + TS) // 4))
    xs = _matmul(o_s.reshape(MS, q_w), att_w_out, 0, tm=tms, tn=512, resid=xs, gate=mod_s(1, 2),
                 rows_per_group=MS, precise=True)

    rw = jnp.pad(moe_router_w[0], ((0, 0), (0, LANES - n_experts)))
    hp, ridx_p, rgate_p = _modulate(xp, norm2_g[1], mod_p(1, 3), mod_p(1, 4), rows_per_group=TP, tm=256,
                                    out_dtype=F32, router_w=rw, n_experts=n_experts)
    hs, ridx_s, rgate_s = _modulate(xs, norm2_g[1], mod_s(1, 3), mod_s(1, 4), rows_per_group=MS, tm=tms,
                                    out_dtype=F32, router_w=rw, n_experts=n_experts)
    h_all = jnp.concatenate([hp, hs], axis=0)
    ridx = jnp.concatenate([ridx_p, ridx_s], axis=0)[:, :MOE_TOPK]
    rgate = jnp.concatenate([rgate_p, rgate_s], axis=0)
    drop_layer = lambda w: w.reshape(w.shape[1:])
    y_slots, slot_of = _moe(h_all, ridx, drop_layer(moe_w1), drop_layer(moe_w3), drop_layer(moe_w2))
    y_prompt = _combine(y_slots, slot_of, xp, mod_p(1, 5), rgate, final_norm_g, rows_per_group=TP, tm=256, row0=0)
    y_sample = _combine(y_slots, slot_of, xs, mod_s(1, 5), rgate, final_norm_g, rows_per_group=MS, tm=tms,
                        row0=MP)

    f32 = lambda a: a.astype(F32)
    return (y_prompt.reshape(BP, TP, D), y_sample.reshape(BS, TS, D),
            s_p[None], conv_p[None], s_s[None], conv_s[None],
            f32(k_p).reshape(1, BP, TP, G, dh), f32(v_p).reshape(1, BP, TP, G, dh), f32(ki_p).reshape(1, BP, TP, di),
            f32(k_s).reshape(1, BS, TS, G, dh), f32(v_s).reshape(1, BS, TS, G, dh), f32(ki_s).reshape(1, BS, TS, di))
```

```python
import functools

import jax
import jax.numpy as jnp
from jax import lax
from jax.experimental import pallas as pl
from jax.experimental.pallas import tpu as pltpu

F32 = jnp.float32
BF16 = jnp.bfloat16
I32 = jnp.int32
HIGHEST = lax.Precision.HIGHEST

LANES = 128
VMEM_CAP_V7X = 60 * 1024 * 1024
NORM_EPS = 1e-6
GDN_CHUNK = 64
IDX_TOPK = 256
ATT_BLOCK = 128
MOE_TOPK = 2
INT32_MIN = -(2 ** 31)


def _params(semantics, vmem_bytes):
    limit = int(min(max(vmem_bytes, 32 * 1024 * 1024), VMEM_CAP_V7X))
    return pltpu.CompilerParams(dimension_semantics=semantics, vmem_limit_bytes=limit)


def _silu(x):
    return x / (1.0 + jnp.exp(-x))


def _dot_nt(a, b, precision=None):
    return lax.dot_general(a, b, (((1,), (1,)), ((), ())), preferred_element_type=F32, precision=precision)


def _dot_tn(a, b, precision=None):
    return lax.dot_general(a, b, (((0,), (0,)), ((), ())), preferred_element_type=F32, precision=precision)


def _bf(x):
    return x.astype(BF16)


def _hi_lo(x):
    hi = x.astype(BF16)
    return hi, (x - hi.astype(F32)).astype(BF16)


def _dot3(a_hi, a_lo, b_hi, b_lo, dot):
    return dot(a_hi, b_hi) + (dot(a_lo, b_hi) + dot(a_hi, b_lo))


def _modulate_kernel(*refs, modulated, n_experts):
    it = iter(refs)
    x_ref, g_ref = next(it), next(it)
    sh_ref = sc_ref = rw_ref = None
    if modulated:
        sh_ref, sc_ref = next(it), next(it)
    if n_experts:
        rw_ref = next(it)
    o_ref = next(it)
    x = x_ref[...]
    y = x * lax.rsqrt(jnp.mean(x * x, axis=-1, keepdims=True) + NORM_EPS) * g_ref[...]
    if modulated:
        y = y * (1.0 + sc_ref[0]) + sh_ref[0]
    o_ref[...] = y.astype(o_ref.dtype)
    if n_experts:
        ridx_ref, rgate_ref = next(it), next(it)
        logits = jnp.dot(y, rw_ref[...], preferred_element_type=F32, precision=HIGHEST)
        lane = lax.broadcasted_iota(I32, logits.shape, 1)
        l1 = jnp.where(lane < n_experts, logits, -jnp.inf)
        m1 = jnp.max(l1, axis=1, keepdims=True)
        i1 = jnp.min(jnp.where(l1 == m1, lane, LANES), axis=1, keepdims=True)
        l2 = jnp.where(lane == i1, -jnp.inf, l1)
        m2 = jnp.max(l2, axis=1, keepdims=True)
        i2 = jnp.min(jnp.where(l2 == m2, lane, LANES), axis=1, keepdims=True)
        e1 = jnp.exp(m2 - m1)
        den = 1.0 + e1
        ridx_ref[...] = jnp.where(lane == 0, i1, jnp.where(lane == 1, i2, 0))
        rgate_ref[...] = jnp.where(lane == 0, 1.0 / den, jnp.where(lane == 1, e1 / den, 0.0))


def _modulate(x, g, shift, scale, *, rows_per_group, tm, out_dtype, router_w=None, n_experts=0):
    M, D = x.shape
    modulated = shift is not None
    tiles_per_group = rows_per_group // tm
    in_specs = [pl.BlockSpec((tm, D), lambda i: (i, 0)), pl.BlockSpec((1, D), lambda i: (0, 0))]
    args = [x, g.reshape(1, D)]
    if modulated:
        R = shift.shape[1]
        spec = pl.BlockSpec((1, R, D), lambda i: (i // tiles_per_group, 0, 0))
        in_specs += [spec, spec]
        args += [shift, scale]
    out_shape = [jax.ShapeDtypeStruct((M, D), out_dtype)]
    out_specs = [pl.BlockSpec((tm, D), lambda i: (i, 0))]
    if n_experts:
        in_specs.append(pl.BlockSpec((D, LANES), lambda i: (0, 0)))
        args.append(router_w)
        out_shape += [jax.ShapeDtypeStruct((M, LANES), I32), jax.ShapeDtypeStruct((M, LANES), F32)]
        out_specs += [pl.BlockSpec((tm, LANES), lambda i: (i, 0))] * 2
    vmem = 2 * tm * D * (4 + jnp.dtype(out_dtype).itemsize) + 6 * tm * D * 4 + 16 * D * 4
    res = pl.pallas_call(
        functools.partial(_modulate_kernel, modulated=modulated, n_experts=n_experts),
        grid=(M // tm,), in_specs=in_specs, out_specs=out_specs, out_shape=out_shape,
        compiler_params=_params(("arbitrary",), vmem), name="modulate")(*args)
    return res if n_experts else res[0]


def _mm_kernel(*refs, a_silu, head_major, has_bias, has_resid, precise):
    it = iter(refs)
    a_ref, w_ref = next(it), next(it)
    b_ref = next(it) if has_bias else None
    r_ref, gt_ref = (next(it), next(it)) if has_resid else (None, None)
    o_ref = next(it)
    o2_ref = next(it) if head_major == "both" else None
    wbf_ref = next(it)
    wlo_ref = next(it) if precise else None

    @pl.when(pl.program_id(1) == 0)
    def _():
        if precise:
            wbf_ref[...], wlo_ref[...] = _hi_lo(w_ref[0])
        else:
            wbf_ref[...] = w_ref[0].astype(BF16)

    a = a_ref[...]
    if a_silu:
        a = _silu(a)
    dot = functools.partial(jnp.dot, preferred_element_type=F32)
    if precise:
        acc = _dot3(*_hi_lo(a), wbf_ref[...], wlo_ref[...], dot)
    else:
        acc = dot(a.astype(BF16), wbf_ref[...])
    if has_bias:
        acc = acc + b_ref[0]
    if has_resid:
        acc = r_ref[...] + gt_ref[0] * acc
    hm_ref = o2_ref if head_major == "both" else (o_ref if head_major else None)
    if hm_ref is not None:
        for j in range(hm_ref.shape[0]):
            hm_ref[j] = acc[:, j * LANES:(j + 1) * LANES].astype(hm_ref.dtype)
    if head_major is not True:
        o_ref[...] = acc.astype(o_ref.dtype)


def _matmul(a, w, wl, *, col0=0, ncols=None, tm, tn, out_dtype=F32, head_major=False, a_silu=False,
            bias=None, resid=None, gate=None, rows_per_group=None, k_parts=1, k_part=0, precise=False):
    M, K = a.shape
    K = K // k_parts
    ncols = w.shape[2] - col0 if ncols is None else ncols
    assert col0 % tn == 0 and M % tm == 0
    n_tiles, off = pl.cdiv(ncols, tn), col0 // tn
    in_specs = [pl.BlockSpec((tm, K), lambda n, m: (m, k_part)),
                pl.BlockSpec((1, K, tn), lambda n, m: (wl, k_part, n + off))]
    args = [a, w]
    if bias is not None:
        in_specs.append(pl.BlockSpec((1, 1, tn), lambda n, m: (wl, 0, n + off)))
        args.append(bias)
    if resid is not None:
        tiles_per_group = rows_per_group // tm
        in_specs += [pl.BlockSpec((tm, tn), lambda n, m: (m, n)),
                     pl.BlockSpec((1, gate.shape[1], tn), lambda n, m: (m // tiles_per_group, 0, n))]
        args += [resid, gate]
    hm_spec = pl.BlockSpec((tn // LANES, tm, LANES), lambda n, m: (n, m, 0))
    rows_spec = pl.BlockSpec((tm, tn), lambda n, m: (m, n))
    if head_major:
        assert ncols % tn == 0
    if head_major is True:
        out_shape, out_spec = jax.ShapeDtypeStruct((ncols // LANES, M, LANES), out_dtype), hm_spec
    elif head_major == "both":
        out_shape = [jax.ShapeDtypeStruct((M, ncols), out_dtype), jax.ShapeDtypeStruct((ncols // LANES, M, LANES), BF16)]
        out_spec = [rows_spec, hm_spec]
    else:
        out_shape, out_spec = jax.ShapeDtypeStruct((M, ncols), out_dtype), rows_spec
    osz = jnp.dtype(out_dtype).itemsize
    n_w = 2 if precise else 1
    vmem = (2 * tm * K * a.dtype.itemsize + 2 * K * tn * 4 + n_w * K * tn * 2 + 2 * tm * tn * osz
            + 4 * tm * tn * 4 + (2 * tm * tn * 4 if resid is not None else 0) + (K * tn * 4 if precise else 0))
    return pl.pallas_call(
        functools.partial(_mm_kernel, a_silu=a_silu, head_major=head_major, has_bias=bias is not None,
                          has_resid=resid is not None, precise=precise),
        grid=(n_tiles, M // tm), in_specs=in_specs, out_specs=out_spec, out_shape=out_shape,
        scratch_shapes=[pltpu.VMEM((K, tn), BF16)] * n_w,
        compiler_params=_params(("arbitrary", "arbitrary"), vmem), name="matmul")(*args)


def _swiglu_kernel(*refs, grouped, precise, sub):
    if grouped:
        fresh_ref, rows_ref = refs[1], refs[2]
        refs = refs[4:]
    a_ref, w1_ref, w3_ref, o_ref, w1bf_ref, w3bf_ref = refs[:6]
    w1lo_ref, w3lo_ref = refs[6:] if precise else (None, None)
    m = pl.program_id(1)
    tm = a_ref.shape[0]
    fresh = (fresh_ref[m] == 1) if grouped else (m == 0)
    dot = functools.partial(jnp.dot, preferred_element_type=F32)

    @pl.when(fresh)
    def _():
        if precise:
            w1bf_ref[...], w1lo_ref[...] = _hi_lo(w1_ref[0])
            w3bf_ref[...], w3lo_ref[...] = _hi_lo(w3_ref[0])
        else:
            w1bf_ref[...] = w1_ref[0].astype(BF16)
            w3bf_ref[...] = w3_ref[0].astype(BF16)

    def compute(n_rows):
        if n_rows:
            a = a_ref[0:n_rows, :]
            if precise:
                a_hi, a_lo = _hi_lo(a)
                h1 = _dot3(a_hi, a_lo, w1bf_ref[...], w1lo_ref[...], dot)
                h3 = _dot3(a_hi, a_lo, w3bf_ref[...], w3lo_ref[...], dot)
            else:
                h1 = dot(a, w1bf_ref[...])
                h3 = dot(a, w3bf_ref[...])
            o_ref[0:n_rows, :] = (_silu(h1) * h3).astype(o_ref.dtype)
        if n_rows < tm:
            o_ref[n_rows:tm, :] = jnp.zeros((tm - n_rows, o_ref.shape[1]), o_ref.dtype)

    if grouped:
        rows = rows_ref[m]
        for n_rows in range(sub, tm + 1, sub):
            pl.when((rows > n_rows - sub) & (rows <= n_rows))(functools.partial(compute, n_rows))
        pl.when(rows == 0)(functools.partial(compute, 0))
    else:
        compute(tm)


def _swiglu_in(a, w1, w3, wl, *, tm, tn, tile_tables=None, precise=False, sub=None):
    M, K = a.shape
    N = w1.shape[2]
    n_tiles = pl.cdiv(N, tn)
    grouped = tile_tables is not None
    if grouped:
        a_map = lambda n, m, te, tf, tr, tt: (tt[m], 0)
        w_map = lambda n, m, te, tf, tr, tt: (te[m], 0, n)
        o_map = lambda n, m, te, tf, tr, tt: (m, n)
    else:
        a_map = lambda n, m: (m, 0)
        w_map = lambda n, m: (wl, 0, n)
        o_map = lambda n, m: (m, n)
    in_specs = [pl.BlockSpec((tm, K), a_map), pl.BlockSpec((1, K, tn), w_map), pl.BlockSpec((1, K, tn), w_map)]
    out_spec = pl.BlockSpec((tm, tn), o_map)
    n_w = 4 if precise else 2
    scratch = [pltpu.VMEM((K, tn), BF16)] * n_w
    vmem = (2 * tm * K * a.dtype.itemsize + 4 * K * tn * 4 + n_w * K * tn * 2 + 2 * tm * tn * 4 + 6 * tm * tn * 4
            + (K * tn * 4 if precise else 0))
    kern = functools.partial(_swiglu_kernel, grouped=grouped, precise=precise, sub=sub)
    out_shape = jax.ShapeDtypeStruct((M, N), F32 if precise else BF16)
    cp = _params(("arbitrary", "arbitrary"), vmem)
    if grouped:
        gs = pltpu.PrefetchScalarGridSpec(num_scalar_prefetch=4, grid=(n_tiles, M // tm), in_specs=in_specs,
                                          out_specs=out_spec, scratch_shapes=scratch)
        return pl.pallas_call(kern, grid_spec=gs, out_shape=out_shape, compiler_params=cp,
                              name="moe_swiglu_in")(*tile_tables, a, w1, w3)
    return pl.pallas_call(kern, grid=(n_tiles, M // tm), in_specs=in_specs, out_specs=out_spec,
                          out_shape=out_shape, scratch_shapes=scratch, compiler_params=cp,
                          name="swiglu_in")(a, w1, w3)


def _expert_out_kernel(te_ref, fresh_ref, rows_ref, tt_ref, a_ref, w_ref, o_ref, *, sub):
    m, k = pl.program_id(0), pl.program_id(1)
    rows = rows_ref[m]
    tm = a_ref.shape[0]

    @pl.when(k == 0)
    def _():
        o_ref[...] = jnp.zeros_like(o_ref)

    def compute(n_rows):
        o_ref[0:n_rows, :] += jnp.dot(a_ref[0:n_rows, :], w_ref[0].astype(BF16), preferred_element_type=F32)

    for n_rows in range(sub, tm + 1, sub):
        pl.when((rows > n_rows - sub) & (rows <= n_rows))(functools.partial(compute, n_rows))


def _expert_out(h, w2, tile_tables, *, tm, tk, sub):
    S, F = h.shape
    N = w2.shape[2]
    n_k = F // tk
    k_eff = lambda m, k, tr: jnp.where(tr[m] > 0, k, n_k - 1)
    gs = pltpu.PrefetchScalarGridSpec(
        num_scalar_prefetch=4, grid=(S // tm, n_k),
        in_specs=[pl.BlockSpec((tm, tk), lambda m, k, te, tf, tr, tt: (tt[m], k_eff(m, k, tr))),
                  pl.BlockSpec((1, tk, N), lambda m, k, te, tf, tr, tt: (te[m], k_eff(m, k, tr), 0))],
        out_specs=pl.BlockSpec((tm, N), lambda m, k, te, tf, tr, tt: (m, 0)))
    vmem = 2 * tm * tk * 2 + 2 * tk * N * 4 + tk * N * 2 + 2 * tm * N * 4 + 2 * sub * N * 4
    return pl.pallas_call(functools.partial(_expert_out_kernel, sub=sub), grid_spec=gs,
                          out_shape=jax.ShapeDtypeStruct((S, N), F32),
                          compiler_params=_params(("arbitrary", "arbitrary"), vmem),
                          name="moe_expert_out")(*tile_tables, h, w2)


def _gdn_gates_kernel(ba_ref, alog_ref, dtb_ref, beta_ref, gc_ref, *, chunk, t_valid, n_heads):
    x = ba_ref[...]
    tc = x.shape[0]
    row = lax.broadcasted_iota(I32, (tc, tc), 0)
    col = lax.broadcasted_iota(I32, (tc, tc), 1)
    valid = (lax.broadcasted_iota(I32, x.shape, 0) % chunk) < t_valid
    beta_ref[...] = jnp.where(valid, 1.0 / (1.0 + jnp.exp(-x)), 0.0)
    y = x + dtb_ref[...]
    softplus = jnp.maximum(y, 0.0) + jnp.log1p(jnp.exp(-jnp.abs(y)))
    g = jnp.where(valid, -jnp.exp(alog_ref[...]) * softplus, 0.0)
    same_chunk_lower = ((row // chunk) == (col // chunk)) & (row >= col)
    gc_ref[...] = jnp.dot(jnp.where(same_chunk_lower, 1.0, 0.0), g, preferred_element_type=F32,
                          precision=HIGHEST)


def _gdn_gates(ba, a_log, dt_bias, *, t_valid, n_heads, tc):
    M = ba.shape[0]
    pad = lambda v: jnp.zeros((1, LANES), F32).at[0, n_heads:2 * n_heads].set(v.astype(F32))
    spec = pl.BlockSpec((tc, LANES), lambda i: (i, 0))
    vec = pl.BlockSpec((1, LANES), lambda i: (0, 0))
    return pl.pallas_call(
        functools.partial(_gdn_gates_kernel, chunk=GDN_CHUNK, t_valid=t_valid, n_heads=n_heads),
        grid=(M // tc,), in_specs=[spec, vec, vec], out_specs=[spec, spec],
        out_shape=[jax.ShapeDtypeStruct((M, LANES), F32)] * 2,
        compiler_params=_params(("arbitrary",), 0), name="gdn_gates")(ba, pad(a_log), pad(dt_bias))


def _gdn_kernel(q_ref, k_ref, v_ref, z_ref, cq_ref, ck_ref, cv_ref, wq_ref, wk_ref, wv_ref,
                bcol_ref, gcol_ref, grow_ref, s0_ref, onorm_ref, o_ref, sfin_ref,
                xq_ref, xk_ref, xv_ref, s_ref, *, hb, n_taps, dk, precise):
    c = pl.program_id(2)
    C = q_ref.shape[1]
    lead = xq_ref.shape[1] - C

    @pl.when(c == 0)
    def _():
        xq_ref[:, 0:lead, :] = cq_ref[:, 0]
        xk_ref[:, 0:lead, :] = ck_ref[:, 0]
        xv_ref[:, 0:lead, :] = cv_ref[:, 0]
        s_ref[...] = s0_ref[0]

    row = lax.broadcasted_iota(I32, (hb, C, C), 1)
    col = lax.broadcasted_iota(I32, (hb, C, C), 2)
    tri, stri = row >= col, row > col
    eye = jnp.where(row == col, 1.0, 0.0)
    def mxu(spec, a, b, three_pass):
        dot = lambda x, y: jnp.einsum(spec, x, y, preferred_element_type=F32)
        return _dot3(*_hi_lo(a), *_hi_lo(b), dot) if three_pass else dot(_bf(a), _bf(b))

    bmm = lambda a, b: mxu("hik,hkj->hij", a, b, precise)
    bmm3 = lambda a, b: mxu("hik,hkj->hij", a, b, True)

    def conv(x_ref, src_ref, w_ref):
        x_ref[:, lead:lead + C, :] = src_ref[...]
        first = lead - n_taps + 1
        y = x_ref[:, first:first + C, :] * w_ref[:, 0:1, :]
        for t in range(1, n_taps):
            y = y + x_ref[:, first + t:first + t + C, :] * w_ref[:, t:t + 1, :]
        x_ref[:, 0:lead, :] = x_ref[:, C:C + lead, :]
        return _silu(y)

    q = conv(xq_ref, q_ref, wq_ref)
    k = conv(xk_ref, k_ref, wk_ref)
    v = conv(xv_ref, v_ref, wv_ref)
    q = q * lax.rsqrt(jnp.sum(q * q, axis=-1, keepdims=True) + NORM_EPS) * (dk ** -0.5)
    k = k * lax.rsqrt(jnp.sum(k * k, axis=-1, keepdims=True) + NORM_EPS)
    bcol, gcol = bcol_ref[0], gcol_ref[0]
    beta = jnp.stack([bcol[:, j:j + 1] for j in range(hb)])
    gc = jnp.stack([gcol[:, j:j + 1] for j in range(hb)])
    gr = grow_ref[0, 0][:, None, :]
    decay = jnp.where(tri, jnp.exp(jnp.where(tri, gc - gr, 0.0)), 0.0)
    kb = k * beta
    eg = jnp.exp(gc)
    kq = mxu("hid,hjd->hij", jnp.concatenate([kb, q], axis=1), k, precise)
    a_mat = jnp.where(stri, kq[:, :C] * decay, 0.0)
    attn = jnp.where(tri, kq[:, C:] * decay, 0.0)
    inv = eye - a_mat
    pw = a_mat
    for _ in range(max(C.bit_length() - 2, 0)):
        pw = bmm3(pw, pw)
        inv = inv + bmm3(inv, pw)
    dv = v.shape[2]
    sol = bmm(inv, jnp.concatenate([v * beta, kb * eg], axis=2))
    u, w = sol[:, :, :dv], sol[:, :, dv:]
    s = s_ref[...]
    ws_qs = bmm(jnp.concatenate([w, q * eg], axis=1), s)
    v_new = u - ws_qs[:, :C]
    o = ws_qs[:, C:] + bmm(attn, v_new)
    g_last = gc[:, C - 1:C, :]
    k_dec = k * jnp.exp(g_last - gc)
    if precise:
        (k_hi, k_lo), (v_hi, v_lo) = _hi_lo(k_dec), _hi_lo(v_new)
        kv = [_dot3(k_hi[j], k_lo[j], v_hi[j], v_lo[j], _dot_tn) for j in range(hb)]
    else:
        k_bf, v_bf = _bf(k_dec), _bf(v_new)
        kv = [_dot_tn(k_bf[j], v_bf[j]) for j in range(hb)]
    s_new = s * jnp.exp(g_last) + jnp.stack(kv)
    s_ref[...] = s_new
    sfin_ref[0] = s_new
    o = o * lax.rsqrt(jnp.mean(o * o, axis=-1, keepdims=True) + NORM_EPS) * onorm_ref[...]
    o = (o * _silu(z_ref[...])).astype(o_ref.dtype)
    for j in range(hb):
        o_ref[:, j * LANES:(j + 1) * LANES] = o[j]


def _gdn_core(proj_hm, ba, conv_buf, s0, conv_w, a_log, dt_bias, o_norm, *, batch, n_chunks, t_valid, hb,
              precise):
    H = s0.shape[1]
    dk = s0.shape[2]
    C = GDN_CHUNK
    M = proj_hm.shape[1]
    n_taps = conv_w.shape[0]
    lead = 8
    nhb = H // hb
    beta, gc = _gdn_gates(ba, a_log, dt_bias, t_valid=t_valid, n_heads=H, tc=min(M, 256))
    beta_col = beta[:, :H].reshape(M, nhb, hb).transpose(1, 0, 2)
    gc_col = gc[:, H:2 * H].reshape(M, nhb, hb).transpose(1, 0, 2)
    gc_row = gc[:, H:2 * H].reshape(M // C, C, nhb, hb).transpose(2, 0, 3, 1)
    cb = conv_buf.astype(F32).reshape(batch, n_taps - 1, 3 * H, LANES).transpose(2, 0, 1, 3)
    cb = jnp.pad(cb, ((0, 0), (0, 0), (lead - (n_taps - 1), 0), (0, 0)))
    cw = jnp.pad(conv_w.astype(F32).reshape(n_taps, 3 * H, LANES).transpose(1, 0, 2),
                 ((0, 0), (0, 8 - n_taps), (0, 0)))

    def tile_spec(section):
        return pl.BlockSpec((hb, C, LANES), lambda b, h, c: (section * nhb + h, b * n_chunks + c, 0))

    def cb_spec(section):
        return pl.BlockSpec((hb, 1, lead, LANES), lambda b, h, c: (section * nhb + h, b, 0, 0))

    def cw_spec(section):
        return pl.BlockSpec((hb, 8, LANES), lambda b, h, c: (section * nhb + h, 0, 0))

    col_spec = pl.BlockSpec((1, C, hb), lambda b, h, c: (h, b * n_chunks + c, 0))
    in_specs = [tile_spec(0), tile_spec(1), tile_spec(2), tile_spec(3), cb_spec(0), cb_spec(1), cb_spec(2),
                cw_spec(0), cw_spec(1), cw_spec(2), col_spec, col_spec,
                pl.BlockSpec((1, 1, hb, C), lambda b, h, c: (h, b * n_chunks + c, 0, 0)),
                pl.BlockSpec((1, hb, dk, LANES), lambda b, h, c: (b, h, 0, 0)),
                pl.BlockSpec((1, LANES), lambda b, h, c: (0, 0))]
    out_specs = [pl.BlockSpec((C, hb * LANES), lambda b, h, c: (b * n_chunks + c, h)),
                 pl.BlockSpec((1, hb, dk, LANES), lambda b, h, c: (b, h, 0, 0))]
    out_shape = [jax.ShapeDtypeStruct((M, H * LANES), F32 if precise else BF16),
                 jax.ShapeDtypeStruct(s0.shape, F32)]
    scratch = [pltpu.VMEM((hb, C + lead, LANES), F32)] * 3 + [pltpu.VMEM((hb, dk, LANES), F32)]
    return pl.pallas_call(
        functools.partial(_gdn_kernel, hb=hb, n_taps=n_taps, dk=dk, precise=precise),
        grid=(batch, nhb, n_chunks), in_specs=in_specs, out_specs=out_specs, out_shape=out_shape,
        scratch_shapes=scratch, compiler_params=_params(("arbitrary",) * 3, 0), name="gdn_delta")(
            proj_hm, proj_hm, proj_hm, proj_hm, cb, cb, cb, cw, cw, cw, beta_col, gc_col, gc_row,
            s0.astype(F32), o_norm.reshape(1, LANES).astype(F32))


def _gdn_mixer(h, x, gate, conv_buf, s0, w_in, w_ba, conv_w, a_log, dt_bias, o_norm, w_out, *,
               batch, seq, rows_per_group, tm, precise):
    H = s0.shape[1]
    n_taps = conv_w.shape[0]
    C = GDN_CHUNK
    M = batch * seq
    qkvz = 4 * H * LANES
    proj_hm = _matmul(h, w_in, 0, ncols=qkvz, tm=tm, tn=512, head_major=True, precise=precise)
    ba = _matmul(h, w_ba, 0, tm=tm, tn=LANES, precise=precise)
    keep = min(seq, n_taps - 1)
    last_rows = proj_hm.reshape(4 * H, batch, seq, LANES)[:3 * H, :, seq - keep:]
    if seq % C == 0:
        n_chunks, t_valid, padded = seq // C, C, proj_hm
        ba_p = ba
    else:
        n_chunks, t_valid = 1, seq
        padded = jnp.pad(proj_hm.reshape(4 * H, batch, seq, LANES), ((0, 0), (0, 0), (0, C - seq), (0, 0)))
        padded = padded.reshape(4 * H, batch * C, LANES)
        ba_p = jnp.pad(ba.reshape(batch, seq, LANES), ((0, 0), (0, C - seq), (0, 0))).reshape(batch * C, LANES)
    o_rows, s_fin = _gdn_core(padded, ba_p, conv_buf, s0, conv_w, a_log, dt_bias, o_norm, batch=batch,
                            n_chunks=n_chunks, t_valid=t_valid, hb=min(16, H), precise=precise)
    o_rows = o_rows.reshape(batch, n_chunks * C, H * LANES)[:, :seq].reshape(M, H * LANES)
    x_new = _matmul(o_rows, w_out, 0, tm=tm, tn=512, resid=x, gate=gate, rows_per_group=rows_per_group,
                    precise=precise)
    new_buf = last_rows.transpose(1, 2, 0, 3).reshape(batch, keep, 3 * H * LANES)
    if keep < n_taps - 1:
        new_buf = jnp.concatenate([conv_buf.astype(F32)[:, keep:], new_buf], axis=1)
    return x_new, new_buf, s_fin


def _ordered_key(score):
    bits = pltpu.bitcast(score + 0.0, I32)
    return bits ^ ((bits >> 31) & 0x7FFFFFFF)


def _topk_select(keys, col, k):
    count = lambda m: jnp.sum(jnp.sum(m.astype(I32), axis=2, keepdims=True), axis=0, keepdims=True)
    n_rows = keys.shape[1]
    n_cols = keys.shape[0] * keys.shape[2]
    zero = jnp.zeros((1, n_rows, 1), I32)
    thr = jnp.where(count(keys >= zero) >= k, zero, jnp.full((1, n_rows, 1), INT32_MIN, I32))

    def bit_step(i, thr):
        cand = thr | (1 << (30 - i))
        return jnp.where(count(keys >= cand) >= k, cand, thr)

    thr = lax.fori_loop(0, 31, bit_step, thr)
    above = keys > thr
    tied = keys == thr
    need = k - count(above)
    n_bits = n_cols.bit_length()

    def tie_step(i, bound):
        cand = bound + (1 << (n_bits - 1 - i))
        ok = (cand <= n_cols) & (count(tied & (col < cand)) <= need)
        return jnp.where(ok, cand, bound)

    bound = lax.fori_loop(0, n_bits, tie_step, zero)
    return above | (tied & (col < bound))


def _dsa_prompt_kernel(qi_ref, wi_ref, ki_ref, q_ref, k_ref, v_ref, o_ref, sc_ref, bias_ref, ohm_ref, *,
                       hc, idx_scale, att_scale, topk, group, kt):
    i = pl.program_id(1)
    n_q, seq = sc_ref.shape
    n_chunks = qi_ref.shape[0] // hc
    n_live = (i * n_q + n_q - 1) // kt + 1

    def body(width):
        ki = ki_ref[0, 0:width, :]
        sc_ref[:, 0:width] = jnp.zeros((n_q, width), F32)

        def chunk(c, carry):
            lhs = qi_ref[pl.ds(c * hc, hc)].reshape(hc * n_q, qi_ref.shape[2])
            d = jnp.maximum(_dot_nt(lhs, ki), 0.0).reshape(hc, n_q, width)
            w = wi_ref[c]
            acc = d[0] * w[:, 0:1]
            for j in range(1, hc):
                acc = acc + d[j] * w[:, j:j + 1]
            sc_ref[:, 0:width] += acc
            return carry

        lax.fori_loop(0, n_chunks, chunk, 0)
        col = lax.broadcasted_iota(I32, (n_q, width), 1)
        allowed = col <= i * n_q + lax.broadcasted_iota(I32, (n_q, width), 0)
        keys = jnp.where(allowed, _ordered_key(sc_ref[:, 0:width] * idx_scale), INT32_MIN)
        sel = _topk_select(keys[None], col[None], topk)[0] & allowed
        bias_ref[:, 0:width] = jnp.where(sel, 0.0, -jnp.inf)

        def grp(g, carry):
            qg = q_ref[pl.ds(g * group, group)].reshape(group * n_q, q_ref.shape[2])
            s = _dot_nt(qg, k_ref[g, 0:width, :]) * att_scale
            s = s.reshape(group, n_q, width) + bias_ref[:, 0:width][None]
            p = jnp.exp(s - jnp.max(s, axis=-1, keepdims=True))
            l = jnp.sum(p, axis=-1, keepdims=True)
            o = jnp.dot(p.reshape(group * n_q, width).astype(BF16), v_ref[g, 0:width, :],
                        preferred_element_type=F32)
            ohm_ref[pl.ds(g * group, group)] = (o.reshape(group, n_q, o.shape[-1]) / l).astype(ohm_ref.dtype)
            return carry

        lax.fori_loop(0, k_ref.shape[0], grp, 0)

    for t in range(seq // kt):
        pl.when(n_live == t + 1)(functools.partial(body, (t + 1) * kt))
    dh = ohm_ref.shape[2]
    for h in range(ohm_ref.shape[0]):
        o_ref[:, h * dh:(h + 1) * dh] = ohm_ref[h]


def _dsa_prompt(q_hm, qi_hm, wi, ki, kv_hm, *, batch, seq, topk):
    HQ, M, dh = q_hm.shape
    HI = qi_hm.shape[0]
    G = kv_hm.shape[0] // 2
    nq = ATT_BLOCK
    nb = seq // nq
    hc = 4
    kt = 512 if seq % 512 == 0 else seq
    wi_c = wi.reshape(M, HI // hc, hc).transpose(1, 0, 2)
    di = qi_hm.shape[2]
    group = HQ // G
    kern = functools.partial(_dsa_prompt_kernel, hc=hc, idx_scale=float(di ** -0.5 * HI ** -0.5),
                             att_scale=float(dh ** -0.5), topk=topk, group=group, kt=kt)
    vmem = (2 * HI * nq * di * 2 + 5 * HQ * nq * dh * 2 + 4 * G * seq * dh * 2 + 2 * seq * di * 2
            + 8 * nq * seq * 4 + 3 * hc * nq * seq * 4 + 4 * group * nq * seq * 4 + (4 << 20))
    return pl.pallas_call(
        kern, grid=(batch, nb),
        in_specs=[pl.BlockSpec((HI, nq, di), lambda b, i: (0, b * nb + i, 0)),
                  pl.BlockSpec((HI // hc, nq, hc), lambda b, i: (0, b * nb + i, 0)),
                  pl.BlockSpec((1, seq, di), lambda b, i: (b, 0, 0)),
                  pl.BlockSpec((HQ, nq, dh), lambda b, i: (0, b * nb + i, 0)),
                  pl.BlockSpec((G, seq, dh), lambda b, i: (0, b, 0)),
                  pl.BlockSpec((G, seq, dh), lambda b, i: (1, b, 0))],
        out_specs=pl.BlockSpec((nq, HQ * dh), lambda b, i: (b * nb + i, 0)),
        out_shape=jax.ShapeDtypeStruct((M, HQ * dh), BF16),
        scratch_shapes=[pltpu.VMEM((nq, seq), F32), pltpu.VMEM((nq, seq), F32), pltpu.VMEM((HQ, nq, dh), BF16)],
        compiler_params=_params(("arbitrary", "arbitrary"), vmem), name="dsa_prompt")(
            qi_hm, wi_c, ki, q_hm, kv_hm, kv_hm)


def _dsa_sample_scores_kernel(pt_ref, qi_ref, wi_ref, pool_ref, new_ref, o_ref, *, n_pages, n_tok, idx_scale):
    p = pl.program_id(1)
    page = jnp.where(p < n_pages, pool_ref[0], new_ref[0])
    d = jnp.maximum(_dot3(*_hi_lo(qi_ref[0]), *_hi_lo(page), _dot_nt), 0.0) * wi_ref[0]
    hi = d.shape[0] // n_tok
    o_ref[0] = jnp.sum(d.reshape(n_tok, hi, d.shape[1]), axis=1) * idx_scale


def _dsa_sample_select_kernel(sc_ref, o_ref, *, n_past, topk):
    sc = sc_ref[0]
    col = lax.broadcasted_iota(I32, sc.shape, 1)
    q_pos = n_past + lax.broadcasted_iota(I32, sc.shape, 0)
    allowed = col <= q_pos
    keys = jnp.where(allowed, _ordered_key(sc), INT32_MIN)
    sel = _topk_select(keys[None], col[None], topk)[0] & allowed
    o_ref[0] = jnp.where(sel, 0.0, -jnp.inf)


def _dsa_sample_attn_kernel(pt_ref, q_ref, bias_ref, hmask_ref, kp_ref, vp_ref, kn_ref, vn_ref, o_ref,
                            m_ref, l_ref, acc_ref, *, n_pages, att_scale):
    p = pl.program_id(1)
    n_tok = bias_ref.shape[1]
    heads = q_ref.shape[1] // n_tok

    @pl.when(p == 0)
    def _():
        m_ref[...] = jnp.full_like(m_ref, -jnp.inf)
        l_ref[...] = jnp.zeros_like(l_ref)
        acc_ref[...] = jnp.zeros_like(acc_ref)

    def step(k_ref, v_ref):
        cols = k_ref.shape[1] * k_ref.shape[2]
        k_hi, k_lo = _hi_lo(k_ref[0].reshape(cols, k_ref.shape[3]))
        v_hi, v_lo = _hi_lo(v_ref[0].reshape(cols, v_ref.shape[3]))
        bias = jnp.concatenate([jnp.broadcast_to(bias_ref[0, t:t + 1, :], (heads, cols)) for t in range(n_tok)],
                               axis=0)
        s = _dot3(*_hi_lo(q_ref[0]), k_hi, k_lo, _dot_nt) * att_scale + (bias + hmask_ref[...])
        m_old = m_ref[...]
        m_new = jnp.maximum(m_old, jnp.max(s, axis=-1, keepdims=True))
        m_safe = jnp.where(m_new == -jnp.inf, 0.0, m_new)
        alpha = jnp.exp(m_old - m_safe)
        pr = jnp.exp(s - m_safe)
        l_ref[...] = alpha * l_ref[...] + jnp.sum(pr, axis=-1, keepdims=True)
        pv = _dot3(*_hi_lo(pr), v_hi, v_lo, functools.partial(jnp.dot, preferred_element_type=F32))
        acc_ref[...] = alpha * acc_ref[...] + pv
        m_ref[...] = m_new

    pl.when(p < n_pages)(functools.partial(step, kp_ref, vp_ref))
    pl.when(p == n_pages)(functools.partial(step, kn_ref, vn_ref))

    @pl.when(p == pl.num_programs(1) - 1)
    def _():
        o_ref[0] = (acc_ref[...] / l_ref[...]).astype(o_ref.dtype)


def _dsa_sample(q, qi, wi, ki_new, k_new, v_new, k_pool, v_pool, ki_pool, page_table, *, topk):
    B, T, _ = q.shape
    n_pool, page, G, dh = k_pool.shape
    di = ki_pool.shape[2]
    HI = wi.shape[2]
    HQ = q.shape[2] // dh
    n_rep = HQ // G
    P = page_table.shape[1]
    n_past = P * page
    s_pad = (P + 1) * page
    idx_scale = float(di ** -0.5 * HI ** -0.5)
    pad_rows = lambda a: jnp.pad(a, ((0, 0), (0, page - T)) + ((0, 0),) * (a.ndim - 2))
    last = lambda b, p, pt: pt[b, jnp.minimum(p, P - 1)]

    scores = pl.pallas_call(
        functools.partial(_dsa_sample_scores_kernel, n_pages=P, n_tok=T, idx_scale=idx_scale),
        grid_spec=pltpu.PrefetchScalarGridSpec(
            num_scalar_prefetch=1, grid=(B, P + 1),
            in_specs=[pl.BlockSpec((1, T * HI, di), lambda b, p, pt: (b, 0, 0)),
                      pl.BlockSpec((1, T * HI, 1), lambda b, p, pt: (b, 0, 0)),
                      pl.BlockSpec((1, page, di), lambda b, p, pt: (last(b, p, pt), 0, 0)),
                      pl.BlockSpec((1, page, di), lambda b, p, pt: (b, 0, 0))],
            out_specs=pl.BlockSpec((1, T, page), lambda b, p, pt: (b, 0, p))),
        out_shape=jax.ShapeDtypeStruct((B, T, s_pad), F32),
        compiler_params=_params(("arbitrary", "arbitrary"), 0), name="dsa_sample_scores")(
            page_table, qi.reshape(B, T * HI, di), wi.reshape(B, T * HI, 1), ki_pool, pad_rows(ki_new))

    bias = pl.pallas_call(
        functools.partial(_dsa_sample_select_kernel, n_past=n_past, topk=topk),
        grid=(B,), in_specs=[pl.BlockSpec((1, T, s_pad), lambda b: (b, 0, 0))],
        out_specs=pl.BlockSpec((1, T, s_pad), lambda b: (b, 0, 0)),
        out_shape=jax.ShapeDtypeStruct((B, T, s_pad), F32),
        compiler_params=_params(("arbitrary",), 0), name="dsa_sample_select")(scores)

    rows, cols = T * HQ, page * G
    row_g = (jnp.arange(rows, dtype=I32) % HQ) // n_rep
    hmask = jnp.where(row_g[:, None] == jnp.arange(cols, dtype=I32)[None, :] % G, 0.0, -jnp.inf).astype(F32)
    o = pl.pallas_call(
        functools.partial(_dsa_sample_attn_kernel, n_pages=P, att_scale=float(dh ** -0.5)),
        grid_spec=pltpu.PrefetchScalarGridSpec(
            num_scalar_prefetch=1, grid=(B, P + 1),
            in_specs=[pl.BlockSpec((1, rows, dh), lambda b, p, pt: (b, 0, 0)),
                      pl.BlockSpec((1, T, cols), lambda b, p, pt: (b, 0, p)),
                      pl.BlockSpec((rows, cols), lambda b, p, pt: (0, 0)),
                      pl.BlockSpec((1, page, G, dh), lambda b, p, pt: (last(b, p, pt), 0, 0, 0)),
                      pl.BlockSpec((1, page, G, dh), lambda b, p, pt: (last(b, p, pt), 0, 0, 0)),
                      pl.BlockSpec((1, page, G, dh), lambda b, p, pt: (b, 0, 0, 0)),
                      pl.BlockSpec((1, page, G, dh), lambda b, p, pt: (b, 0, 0, 0))],
            out_specs=pl.BlockSpec((1, rows, dh), lambda b, p, pt: (b, 0, 0)),
            scratch_shapes=[pltpu.VMEM((rows, 1), F32), pltpu.VMEM((rows, 1), F32),
                            pltpu.VMEM((rows, dh), F32)]),
        out_shape=jax.ShapeDtypeStruct((B, rows, dh), F32),
        compiler_params=_params(("arbitrary", "arbitrary"), 0), name="dsa_sample_attn")(
            page_table, q.reshape(B, rows, dh), jnp.repeat(bias, G, axis=2), hmask, k_pool, v_pool,
            pad_rows(k_new), pad_rows(v_new))
    return o.reshape(B, T, HQ * dh)


def _gather_rows_kernel(idx_ref, rows_ref, x_hbm, o_ref, buf_ref, sem, *, tm):
    i = pl.program_id(0)
    base = i * tm
    n_rows = rows_ref[i]

    def copy(r):
        return pltpu.make_async_copy(x_hbm.at[pl.ds(idx_ref[base + r], 1)], buf_ref.at[pl.ds(r, 1)], sem)

    def start(r, c):
        copy(r).start()
        return c

    def wait(r, c):
        copy(r).wait()
        return c

    buf_ref[...] = jnp.zeros_like(buf_ref)
    lax.fori_loop(0, n_rows, start, 0)
    lax.fori_loop(0, n_rows, wait, 0)
    o_ref[...] = buf_ref[...].astype(o_ref.dtype)


def _gather_rows(x, idx, rows, *, tm, out_dtype):
    S = idx.shape[0]
    W = x.shape[1]
    return pl.pallas_call(
        functools.partial(_gather_rows_kernel, tm=tm),
        grid_spec=pltpu.PrefetchScalarGridSpec(
            num_scalar_prefetch=2, grid=(S // tm,),
            in_specs=[pl.BlockSpec(memory_space=pl.ANY)],
            out_specs=pl.BlockSpec((tm, W), lambda i, idx, rows: (i, 0)),
            scratch_shapes=[pltpu.VMEM((tm, W), x.dtype), pltpu.SemaphoreType.DMA]),
        out_shape=jax.ShapeDtypeStruct((S, W), out_dtype),
        compiler_params=_params(("arbitrary",), 6 * tm * W * 4), name="moe_gather")(idx, rows, x)


def _combine_kernel(slot_ref, y_hbm, x_ref, gt_ref, rg_ref, fg_ref, o_ref, buf_ref, sem, *, tm, n_top, row0):
    base = row0 + pl.program_id(0) * tm

    def copy(r, j):
        return pltpu.make_async_copy(y_hbm.at[pl.ds(slot_ref[(base + r) * n_top + j], 1)],
                                     buf_ref.at[j, pl.ds(r, 1)], sem)

    def start(r, c):
        for j in range(n_top):
            copy(r, j).start()
        return c

    def wait(r, c):
        for j in range(n_top):
            copy(r, j).wait()
        return c

    lax.fori_loop(0, tm, start, 0)
    lax.fori_loop(0, tm, wait, 0)
    rg = rg_ref[...]
    moe = buf_ref[0] * rg[:, 0:1]
    for j in range(1, n_top):
        moe = moe + buf_ref[j] * rg[:, j:j + 1]
    x = x_ref[...] + gt_ref[0] * moe
    o_ref[...] = x * lax.rsqrt(jnp.mean(x * x, axis=-1, keepdims=True) + NORM_EPS) * fg_ref[...]


def _combine(y_slots, slot_of, x, gate, route_gate, final_g, *, rows_per_group, tm, row0):
    M, D = x.shape
    tiles_per_group = rows_per_group // tm
    t0 = row0 // tm
    return pl.pallas_call(
        functools.partial(_combine_kernel, tm=tm, n_top=MOE_TOPK, row0=row0),
        grid_spec=pltpu.PrefetchScalarGridSpec(
            num_scalar_prefetch=1, grid=(M // tm,),
            in_specs=[pl.BlockSpec(memory_space=pl.ANY),
                      pl.BlockSpec((tm, D), lambda i, s: (i, 0)),
                      pl.BlockSpec((1, gate.shape[1], D), lambda i, s: (i // tiles_per_group, 0, 0)),
                      pl.BlockSpec((tm, LANES), lambda i, s: (i + t0, 0)),
                      pl.BlockSpec((1, D), lambda i, s: (0, 0))],
            out_specs=pl.BlockSpec((tm, D), lambda i, s: (i, 0)),
            scratch_shapes=[pltpu.VMEM((MOE_TOPK, tm, D), F32), pltpu.SemaphoreType.DMA]),
        out_shape=jax.ShapeDtypeStruct((M, D), F32),
        compiler_params=_params(("arbitrary",), (MOE_TOPK + 8) * tm * D * 4), name="moe_combine")(
            slot_of, y_slots, x, gate, route_gate, final_g.reshape(1, D))


def _tile_tables(rows, expert):
    t = jnp.arange(rows.shape[0], dtype=I32)
    seen = (t[None, :] <= t[:, None]) & (rows[None, :] > 0)
    tile = jnp.max(jnp.where(seen, t[None, :], 0), axis=1)
    expert = expert[tile]
    fresh = jnp.concatenate([jnp.ones((1,), I32), (expert[1:] != expert[:-1]).astype(I32)])
    return expert.astype(I32), fresh, rows.astype(I32), tile.astype(I32)


def _route_tables(expert_idx, n_experts, align, tm):
    N, top = expert_idx.shape
    flat = expert_idx.reshape(-1)
    ids = jnp.arange(n_experts, dtype=I32)
    onehot = (flat[:, None] == ids[None, :]).astype(I32)
    rank = jnp.sum((jnp.cumsum(onehot, axis=0) - 1) * onehot, axis=1)
    counts = jnp.sum(onehot, axis=0)
    blocks = (counts + align - 1) // align
    blk_end = jnp.cumsum(blocks)
    blk_start = blk_end - blocks
    n_blocks = (N * top + n_experts * (align - 1)) // align
    slot_of = (jnp.sum(onehot * blk_start[None, :], axis=1) * align + rank).astype(I32)
    src_row = jnp.zeros((n_blocks * align,), I32).at[slot_of].set(jnp.arange(N * top, dtype=I32) // top)
    b = jnp.arange(n_blocks, dtype=I32)
    blk_expert = jnp.minimum(jnp.sum((blk_end[None, :] <= b[:, None]).astype(I32), axis=1), n_experts - 1)
    blk_rows = jnp.clip(counts[blk_expert] - (b - blk_start[blk_expert]) * align, 0, align)
    blk_rows = jnp.where(b < blk_end[-1], blk_rows, 0)
    per = align // tm
    tile_rows = jnp.clip(jnp.repeat(blk_rows, per) - jnp.tile(jnp.arange(per, dtype=I32) * tm, n_blocks), 0, tm)
    return (src_row, slot_of, _tile_tables(tile_rows, jnp.repeat(blk_expert, per)),
            _tile_tables(blk_rows, blk_expert))


MOE_ALIGN = 1024
MOE_TILE = 512
MOE_SUB = 128


def _moe(h_rows, expert_idx, w1, w3, w2):
    src_row, slot_of, tables, blk_tables = _route_tables(expert_idx, w1.shape[0], MOE_ALIGN, MOE_TILE)
    xg = _gather_rows(h_rows, src_row, tables[2], tm=MOE_TILE, out_dtype=BF16)
    hmid = _swiglu_in(xg, w1, w3, None, tm=MOE_TILE, tn=512, tile_tables=tables, sub=MOE_SUB)
    y = _expert_out(hmid, w2, blk_tables, tm=MOE_ALIGN, tk=256, sub=MOE_SUB)
    return y, slot_of


def _pad_cols(w, width):
    return jnp.pad(w, ((0, 0), (0, 0), (0, width - w.shape[2])))


def kernel(x_prompt, x_sample, state_gdn, state_conv, cache_k, cache_v, cache_kidx, page_table, c_prompt, c_sample, ada_w, ada_b, norm1_g, norm2_g, gdn_w_in, gdn_conv_w, gdn_a_log, gdn_dt_bias, gdn_o_norm, gdn_w_out, att_w_in, att_w_out, ffn_w1, ffn_w3, ffn_w2, moe_router_w, moe_w1, moe_w3, moe_w2, final_norm_g):
    BP, TP, D = x_prompt.shape
    BS, TS, _ = x_sample.shape
    MP, MS = BP * TP, BS * TS
    H = state_gdn.shape[2]
    G, dh = cache_k.shape[3], cache_k.shape[4]
    di = cache_kidx.shape[3]
    HQ = att_w_out.shape[1] // dh
    q_w, kv_w = HQ * dh, G * dh
    HI = (att_w_in.shape[2] - q_w - 2 * kv_w - di) // (di + 1)
    qi_w = HI * di
    n_experts = moe_w1.shape[1]
    tmp = 512 if MP % 512 == 0 else MP
    tms = MS

    n_c = BP + BS
    c_all = jnp.pad(jnp.concatenate([c_prompt, c_sample], axis=0), ((0, -n_c % 16), (0, 0)))
    mods = [_matmul(c_all, ada_w, l, tm=c_all.shape[0], tn=512, a_silu=True, bias=ada_b.reshape(-1, 1, 6 * D),
                    precise=True) for l in range(ada_w.shape[0])]

    def mod_p(l, j):
        return mods[l][:BP, j * D:(j + 1) * D].reshape(BP, 1, D)

    def mod_s(l, j):
        return jnp.repeat(mods[l][BP:n_c, j * D:(j + 1) * D], TS, axis=0).reshape(1, MS, D)

    xp = x_prompt.reshape(MP, D)
    xs = x_sample.reshape(MS, D)

    hp = _modulate(xp, norm1_g[0], mod_p(0, 0), mod_p(0, 1), rows_per_group=TP, tm=256, out_dtype=BF16)
    hs = _modulate(xs, norm1_g[0], mod_s(0, 0), mod_s(0, 1), rows_per_group=MS, tm=tms, out_dtype=F32)
    qkvz = 4 * H * LANES
    w_ba = _pad_cols(gdn_w_in[:, :, qkvz:], LANES)
    zeros_buf = jnp.zeros((BP,) + state_conv.shape[2:], F32)
    zeros_s = jnp.zeros((BP,) + state_gdn.shape[2:], F32)
    gdn = functools.partial(_gdn_mixer, w_in=gdn_w_in, w_ba=w_ba, conv_w=gdn_conv_w[0], a_log=gdn_a_log[0],
                            dt_bias=gdn_dt_bias[0], o_norm=gdn_o_norm[0], w_out=gdn_w_out)
    xp, conv_p, s_p = gdn(hp, xp, mod_p(0, 2), zeros_buf, zeros_s, batch=BP, seq=TP, rows_per_group=TP, tm=tmp,
                          precise=False)
    xs, conv_s, s_s = gdn(hs, xs, mod_s(0, 2), state_conv[0], state_gdn[0], batch=BS, seq=TS,
                          rows_per_group=MS, tm=tms, precise=True)

    def ffn(x, shift, scale, gate, rows_per_group, tm, tmod, precise):
        tn = 256 if precise else 512
        h = _modulate(x, norm2_g[0], shift, scale, rows_per_group=rows_per_group, tm=tmod,
                      out_dtype=F32 if precise else BF16)
        mid = _swiglu_in(h, ffn_w1, ffn_w3, 0, tm=tm, tn=tn, precise=precise)
        k_parts = 2 if mid.shape[1] % (2 * LANES) == 0 else 1
        for part in range(k_parts):
            x = _matmul(mid, ffn_w2, 0, tm=tm, tn=tn, resid=x, gate=gate, rows_per_group=rows_per_group,
                        k_parts=k_parts, k_part=part, precise=precise)
        return x

    xp = ffn(xp, mod_p(0, 3), mod_p(0, 4), mod_p(0, 5), TP, tmp, 256, False)
    xs = ffn(xs, mod_s(0, 3), mod_s(0, 4), mod_s(0, 5), MS, tms, tms, True)

    hp = _modulate(xp, norm1_g[1], mod_p(1, 0), mod_p(1, 1), rows_per_group=TP, tm=256, out_dtype=BF16)
    hs = _modulate(xs, norm1_g[1], mod_s(1, 0), mod_s(1, 1), rows_per_group=MS, tm=tms, out_dtype=F32)
    w_kiwi = _pad_cols(att_w_in[:, :, q_w + 2 * kv_w + qi_w:], 2 * LANES)

    def att_proj(h, tm, head_major, precise):
        q = _matmul(h, att_w_in, 0, col0=0, ncols=q_w, tm=tm, tn=512, out_dtype=F32 if precise else BF16,
                    head_major=head_major, precise=precise)
        kv = _matmul(h, att_w_in, 0, col0=q_w, ncols=2 * kv_w, tm=tm, tn=512,
                     head_major="both" if head_major else False, precise=precise)
        qi = _matmul(h, att_w_in, 0, col0=q_w + 2 * kv_w, ncols=qi_w, tm=tm, tn=512,
                     out_dtype=F32 if precise else BF16, head_major=head_major, precise=precise)
        kiwi = _matmul(h, w_kiwi, 0, tm=tm, tn=2 * LANES, precise=precise)
        return q, kv, qi, kiwi[:, :di], kiwi[:, di:di + HI]

    q, (kv_p, kv_hm), qi, ki_p, wi = att_proj(hp, tmp, True, False)
    k_p, v_p = kv_p[:, :kv_w], kv_p[:, kv_w:]
    o_rows = _dsa_prompt(q, qi, wi, ki_p.astype(BF16).reshape(BP, TP, di), kv_hm, batch=BP, seq=TP,
                         topk=min(IDX_TOPK, TP // 4))
    xp = _matmul(o_rows, att_w_out, 0, tm=tmp, tn=512, resid=xp, gate=mod_p(1, 2), rows_per_group=TP)

    q, kv_s, qi, ki_s, wi = att_proj(hs, tms, False, True)
    k_s, v_s = kv_s[:, :kv_w], kv_s[:, kv_w:]
    n_past = page_table.shape[1] * cache_k.shape[2]
    o_s = _dsa_sample(q.reshape(BS, TS, q_w), qi.reshape(BS, TS, qi_w), wi.reshape(BS, TS, HI),
                      ki_s.reshape(BS, TS, di), k_s.reshape(BS, TS, G, dh), v_s.reshape(BS, TS, G, dh),
                      cache_k.reshape(cache_k.shape[1:]), cache_v.reshape(cache_v.shape[1:]),
                      cache_kidx.reshape(cache_kidx.shape[1:]), page_table,
                      topk=min(IDX_TOPK, (n_past + TS) // 4))
    xs = _matmul(o_s.reshape(MS, q_w), att_w_out, 0, tm=tms, tn=512, resid=xs, gate=mod_s(1, 2),
                 rows_per_group=MS, precise=True)

    rw = jnp.pad(moe_router_w[0], ((0, 0), (0, LANES - n_experts)))
    hp, ridx_p, rgate_p = _modulate(xp, norm2_g[1], mod_p(1, 3), mod_p(1, 4), rows_per_group=TP, tm=256,
                                    out_dtype=F32, router_w=rw, n_experts=n_experts)
    hs, ridx_s, rgate_s = _modulate(xs, norm2_g[1], mod_s(1, 3), mod_s(1, 4), rows_per_group=MS, tm=tms,
                                    out_dtype=F32, router_w=rw, n_experts=n_experts)
    h_all = jnp.concatenate([hp, hs], axis=0)
    ridx = jnp.concatenate([ridx_p, ridx_s], axis=0)[:, :MOE_TOPK]
    rgate = jnp.concatenate([rgate_p, rgate_s], axis=0)
    drop_layer = lambda w: w.reshape(w.shape[1:])
    y_slots, slot_of = _moe(h_all, ridx, drop_layer(moe_w1), drop_layer(moe_w3), drop_layer(moe_w2))
    y_prompt = _combine(y_slots, slot_of, xp, mod_p(1, 5), rgate, final_norm_g, rows_per_group=TP, tm=256, row0=0)
    y_sample = _combine(y_slots, slot_of, xs, mod_s(1, 5), rgate, final_norm_g, rows_per_group=MS, tm=tms,
                        row0=MP)

    f32 = lambda a: a.astype(F32)
    return (y_prompt.reshape(BP, TP, D), y_sample.reshape(BS, TS, D),
            s_p[None], conv_p[None], s_s[None], conv_s[None],
            f32(k_p).reshape(1, BP, TP, G, dh), f32(v_p).reshape(1, BP, TP, G, dh), f32(ki_p).reshape(1, BP, TP, di),
            f32(k_s).reshape(1, BS, TS, G, dh), f32(v_s).reshape(1, BS, TS, G, dh), f32(ki_s).reshape(1, BS, TS, di))
```

```python
import functools

import jax
import jax.numpy as jnp
from jax import lax
from jax.experimental import pallas as pl
from jax.experimental.pallas import tpu as pltpu

F32 = jnp.float32
BF16 = jnp.bfloat16
I32 = jnp.int32
HIGHEST = lax.Precision.HIGHEST

LANES = 128
VMEM_CAP_V7X = 60 * 1024 * 1024
NORM_EPS = 1e-6
GDN_CHUNK = 64
IDX_TOPK = 256
ATT_BLOCK = 128
MOE_TOPK = 2
INT32_MIN = -(2 ** 31)


def _params(semantics, vmem_bytes):
    limit = int(min(max(vmem_bytes, 32 * 1024 * 1024), VMEM_CAP_V7X))
    return pltpu.CompilerParams(dimension_semantics=semantics, vmem_limit_bytes=limit)


def _silu(x):
    return x / (1.0 + jnp.exp(-x))


def _dot_nt(a, b, precision=None):
    return lax.dot_general(a, b, (((1,), (1,)), ((), ())), preferred_element_type=F32, precision=precision)


def _dot_tn(a, b, precision=None):
    return lax.dot_general(a, b, (((0,), (0,)), ((), ())), preferred_element_type=F32, precision=precision)


def _bf(x):
    return x.astype(BF16)


def _hi_lo(x):
    hi = x.astype(BF16)
    return hi, (x - hi.astype(F32)).astype(BF16)


def _dot3(a_hi, a_lo, b_hi, b_lo, dot):
    return dot(a_hi, b_hi) + (dot(a_lo, b_hi) + dot(a_hi, b_lo))


def _modulate_kernel(*refs, modulated, n_experts):
    it = iter(refs)
    x_ref, g_ref = next(it), next(it)
    sh_ref = sc_ref = rw_ref = None
    if modulated:
        sh_ref, sc_ref = next(it), next(it)
    if n_experts:
        rw_ref = next(it)
    o_ref = next(it)
    x = x_ref[...]
    y = x * lax.rsqrt(jnp.mean(x * x, axis=-1, keepdims=True) + NORM_EPS) * g_ref[...]
    if modulated:
        y = y * (1.0 + sc_ref[0]) + sh_ref[0]
    o_ref[...] = y.astype(o_ref.dtype)
    if n_experts:
        ridx_ref, rgate_ref = next(it), next(it)
        logits = jnp.dot(y, rw_ref[...], preferred_element_type=F32, precision=HIGHEST)
        lane = lax.broadcasted_iota(I32, logits.shape, 1)
        l1 = jnp.where(lane < n_experts, logits, -jnp.inf)
        m1 = jnp.max(l1, axis=1, keepdims=True)
        i1 = jnp.min(jnp.where(l1 == m1, lane, LANES), axis=1, keepdims=True)
        l2 = jnp.where(lane == i1, -jnp.inf, l1)
        m2 = jnp.max(l2, axis=1, keepdims=True)
        i2 = jnp.min(jnp.where(l2 == m2, lane, LANES), axis=1, keepdims=True)
        e1 = jnp.exp(m2 - m1)
        den = 1.0 + e1
        ridx_ref[...] = jnp.where(lane == 0, i1, jnp.where(lane == 1, i2, 0))
        rgate_ref[...] = jnp.where(lane == 0, 1.0 / den, jnp.where(lane == 1, e1 / den, 0.0))


def _modulate(x, g, shift, scale, *, rows_per_group, tm, out_dtype, router_w=None, n_experts=0):
    M, D = x.shape
    modulated = shift is not None
    tiles_per_group = rows_per_group // tm
    in_specs = [pl.BlockSpec((tm, D), lambda i: (i, 0)), pl.BlockSpec((1, D), lambda i: (0, 0))]
    args = [x, g.reshape(1, D)]
    if modulated:
        R = shift.shape[1]
        spec = pl.BlockSpec((1, R, D), lambda i: (i // tiles_per_group, 0, 0))
        in_specs += [spec, spec]
        args += [shift, scale]
    out_shape = [jax.ShapeDtypeStruct((M, D), out_dtype)]
    out_specs = [pl.BlockSpec((tm, D), lambda i: (i, 0))]
    if n_experts:
        in_specs.append(pl.BlockSpec((D, LANES), lambda i: (0, 0)))
        args.append(router_w)
        out_shape += [jax.ShapeDtypeStruct((M, LANES), I32), jax.ShapeDtypeStruct((M, LANES), F32)]
        out_specs += [pl.BlockSpec((tm, LANES), lambda i: (i, 0))] * 2
    vmem = 2 * tm * D * (4 + jnp.dtype(out_dtype).itemsize) + 6 * tm * D * 4 + 16 * D * 4
    res = pl.pallas_call(
        functools.partial(_modulate_kernel, modulated=modulated, n_experts=n_experts),
        grid=(M // tm,), in_specs=in_specs, out_specs=out_specs, out_shape=out_shape,
        compiler_params=_params(("arbitrary",), vmem), name="modulate")(*args)
    return res if n_experts else res[0]


def _mm_kernel(*refs, a_silu, head_major, has_bias, has_resid, precise):
    it = iter(refs)
    a_ref, w_ref = next(it), next(it)
    b_ref = next(it) if has_bias else None
    r_ref, gt_ref = (next(it), next(it)) if has_resid else (None, None)
    o_ref = next(it)
    o2_ref = next(it) if head_major == "both" else None
    wbf_ref = next(it)
    wlo_ref = next(it) if precise else None

    @pl.when(pl.program_id(1) == 0)
    def _():
        if precise:
            wbf_ref[...], wlo_ref[...] = _hi_lo(w_ref[0])
        else:
            wbf_ref[...] = w_ref[0].astype(BF16)

    a = a_ref[...]
    if a_silu:
        a = _silu(a)
    dot = functools.partial(jnp.dot, preferred_element_type=F32)
    if precise:
        acc = _dot3(*_hi_lo(a), wbf_ref[...], wlo_ref[...], dot)
    else:
        acc = dot(a.astype(BF16), wbf_ref[...])
    if has_bias:
        acc = acc + b_ref[0]
    if has_resid:
        acc = r_ref[...] + gt_ref[0] * acc
    hm_ref = o2_ref if head_major == "both" else (o_ref if head_major else None)
    if hm_ref is not None:
        for j in range(hm_ref.shape[0]):
            hm_ref[j] = acc[:, j * LANES:(j + 1) * LANES].astype(hm_ref.dtype)
    if head_major is not True:
        o_ref[...] = acc.astype(o_ref.dtype)


def _matmul(a, w, wl, *, col0=0, ncols=None, tm, tn, out_dtype=F32, head_major=False, a_silu=False,
            bias=None, resid=None, gate=None, rows_per_group=None, k_parts=1, k_part=0, precise=False):
    M, K = a.shape
    K = K // k_parts
    ncols = w.shape[2] - col0 if ncols is None else ncols
    assert col0 % tn == 0 and M % tm == 0
    n_tiles, off = pl.cdiv(ncols, tn), col0 // tn
    in_specs = [pl.BlockSpec((tm, K), lambda n, m: (m, k_part)),
                pl.BlockSpec((1, K, tn), lambda n, m: (wl, k_part, n + off))]
    args = [a, w]
    if bias is not None:
        in_specs.append(pl.BlockSpec((1, 1, tn), lambda n, m: (wl, 0, n + off)))
        args.append(bias)
    if resid is not None:
        tiles_per_group = rows_per_group // tm
        in_specs += [pl.BlockSpec((tm, tn), lambda n, m: (m, n)),
                     pl.BlockSpec((1, gate.shape[1], tn), lambda n, m: (m // tiles_per_group, 0, n))]
        args += [resid, gate]
    hm_spec = pl.BlockSpec((tn // LANES, tm, LANES), lambda n, m: (n, m, 0))
    rows_spec = pl.BlockSpec((tm, tn), lambda n, m: (m, n))
    if head_major:
        assert ncols % tn == 0
    if head_major is True:
        out_shape, out_spec = jax.ShapeDtypeStruct((ncols // LANES, M, LANES), out_dtype), hm_spec
    elif head_major == "both":
        out_shape = [jax.ShapeDtypeStruct((M, ncols), out_dtype), jax.ShapeDtypeStruct((ncols // LANES, M, LANES), BF16)]
        out_spec = [rows_spec, hm_spec]
    else:
        out_shape, out_spec = jax.ShapeDtypeStruct((M, ncols), out_dtype), rows_spec
    osz = jnp.dtype(out_dtype).itemsize
    n_w = 2 if precise else 1
    vmem = (2 * tm * K * a.dtype.itemsize + 2 * K * tn * 4 + n_w * K * tn * 2 + 2 * tm * tn * osz
            + 4 * tm * tn * 4 + (2 * tm * tn * 4 if resid is not None else 0) + (K * tn * 4 if precise else 0))
    return pl.pallas_call(
        functools.partial(_mm_kernel, a_silu=a_silu, head_major=head_major, has_bias=bias is not None,
                          has_resid=resid is not None, precise=precise),
        grid=(n_tiles, M // tm), in_specs=in_specs, out_specs=out_spec, out_shape=out_shape,
        scratch_shapes=[pltpu.VMEM((K, tn), BF16)] * n_w,
        compiler_params=_params(("arbitrary", "arbitrary"), vmem), name="matmul")(*args)


def _swiglu_kernel(*refs, grouped, precise, sub):
    if grouped:
        fresh_ref, rows_ref = refs[1], refs[2]
        refs = refs[5:]
    a_ref, w1_ref, w3_ref, o_ref, w1bf_ref, w3bf_ref = refs[:6]
    w1lo_ref, w3lo_ref = refs[6:] if precise else (None, None)
    m = pl.program_id(1)
    tm = a_ref.shape[0]
    fresh = (fresh_ref[m] == 1) if grouped else (m == 0)
    dot = functools.partial(jnp.dot, preferred_element_type=F32)

    @pl.when(fresh)
    def _():
        if precise:
            w1bf_ref[...], w1lo_ref[...] = _hi_lo(w1_ref[0])
            w3bf_ref[...], w3lo_ref[...] = _hi_lo(w3_ref[0])
        else:
            w1bf_ref[...] = w1_ref[0].astype(BF16)
            w3bf_ref[...] = w3_ref[0].astype(BF16)

    def compute(n_rows):
        if n_rows:
            a = a_ref[0:n_rows, :]
            if precise:
                a_hi, a_lo = _hi_lo(a)
                h1 = _dot3(a_hi, a_lo, w1bf_ref[...], w1lo_ref[...], dot)
                h3 = _dot3(a_hi, a_lo, w3bf_ref[...], w3lo_ref[...], dot)
            else:
                h1 = dot(a, w1bf_ref[...])
                h3 = dot(a, w3bf_ref[...])
            o_ref[0:n_rows, :] = (_silu(h1) * h3).astype(o_ref.dtype)
        if n_rows < tm:
            o_ref[n_rows:tm, :] = jnp.zeros((tm - n_rows, o_ref.shape[1]), o_ref.dtype)

    if grouped:
        rows = rows_ref[m]
        for n_rows in range(sub, tm + 1, sub):
            pl.when((rows > n_rows - sub) & (rows <= n_rows))(functools.partial(compute, n_rows))
        pl.when(rows == 0)(functools.partial(compute, 0))
    else:
        compute(tm)


def _swiglu_in(a, w1, w3, wl, *, tm, tn, tile_tables=None, precise=False, sub=None):
    M, K = a.shape
    N = w1.shape[2]
    n_tiles = pl.cdiv(N, tn)
    grouped = tile_tables is not None
    if grouped:
        a_map = lambda n, m, te, tf, tr, ta, to: (ta[m], 0)
        w_map = lambda n, m, te, tf, tr, ta, to: (te[m], 0, n)
        o_map = lambda n, m, te, tf, tr, ta, to: (to[m], n)
    else:
        a_map = lambda n, m: (m, 0)
        w_map = lambda n, m: (wl, 0, n)
        o_map = lambda n, m: (m, n)
    in_specs = [pl.BlockSpec((tm, K), a_map), pl.BlockSpec((1, K, tn), w_map), pl.BlockSpec((1, K, tn), w_map)]
    out_spec = pl.BlockSpec((tm, tn), o_map)
    n_w = 4 if precise else 2
    scratch = [pltpu.VMEM((K, tn), BF16)] * n_w
    vmem = (2 * tm * K * a.dtype.itemsize + 4 * K * tn * 4 + n_w * K * tn * 2 + 2 * tm * tn * 4 + 6 * tm * tn * 4
            + (K * tn * 4 if precise else 0))
    kern = functools.partial(_swiglu_kernel, grouped=grouped, precise=precise, sub=sub)
    out_shape = jax.ShapeDtypeStruct((M, N), F32 if precise else BF16)
    cp = _params(("arbitrary", "arbitrary"), vmem)
    if grouped:
        gs = pltpu.PrefetchScalarGridSpec(num_scalar_prefetch=5, grid=(n_tiles, M // tm), in_specs=in_specs,
                                          out_specs=out_spec, scratch_shapes=scratch)
        return pl.pallas_call(kern, grid_spec=gs, out_shape=out_shape, compiler_params=cp,
                              name="moe_swiglu_in")(*tile_tables, a, w1, w3)
    return pl.pallas_call(kern, grid=(n_tiles, M // tm), in_specs=in_specs, out_specs=out_spec,
                          out_shape=out_shape, scratch_shapes=scratch, compiler_params=cp,
                          name="swiglu_in")(a, w1, w3)


def _expert_out_kernel(te_ref, fresh_ref, rows_ref, tt_ref, a_ref, w_ref, o_ref, *, sub):
    m, k = pl.program_id(0), pl.program_id(1)
    rows = rows_ref[m]
    tm = a_ref.shape[0]

    @pl.when(k == 0)
    def _():
        o_ref[...] = jnp.zeros_like(o_ref)

    def compute(n_rows):
        o_ref[0:n_rows, :] += jnp.dot(a_ref[0:n_rows, :], w_ref[0].astype(BF16), preferred_element_type=F32)

    for n_rows in range(sub, tm + 1, sub):
        pl.when((rows > n_rows - sub) & (rows <= n_rows))(functools.partial(compute, n_rows))


def _expert_out(h, w2, tile_tables, *, tm, tk, sub):
    S, F = h.shape
    N = w2.shape[2]
    n_k = F // tk
    k_eff = lambda m, k, tr: jnp.where(tr[m] > 0, k, n_k - 1)
    gs = pltpu.PrefetchScalarGridSpec(
        num_scalar_prefetch=4, grid=(S // tm, n_k),
        in_specs=[pl.BlockSpec((tm, tk), lambda m, k, te, tf, tr, tt: (tt[m], k_eff(m, k, tr))),
                  pl.BlockSpec((1, tk, N), lambda m, k, te, tf, tr, tt: (te[m], k_eff(m, k, tr), 0))],
        out_specs=pl.BlockSpec((tm, N), lambda m, k, te, tf, tr, tt: (m, 0)))
    vmem = 2 * tm * tk * 2 + 2 * tk * N * 4 + tk * N * 2 + 2 * tm * N * 4 + 2 * sub * N * 4
    return pl.pallas_call(functools.partial(_expert_out_kernel, sub=sub), grid_spec=gs,
                          out_shape=jax.ShapeDtypeStruct((S, N), F32),
                          compiler_params=_params(("arbitrary", "arbitrary"), vmem),
                          name="moe_expert_out")(*tile_tables, h, w2)


def _gdn_gates_kernel(ba_ref, alog_ref, dtb_ref, beta_ref, gc_ref, *, chunk, t_valid, n_heads):
    x = ba_ref[...]
    tc = x.shape[0]
    row = lax.broadcasted_iota(I32, (tc, tc), 0)
    col = lax.broadcasted_iota(I32, (tc, tc), 1)
    valid = (lax.broadcasted_iota(I32, x.shape, 0) % chunk) < t_valid
    beta_ref[...] = jnp.where(valid, 1.0 / (1.0 + jnp.exp(-x)), 0.0)
    y = x + dtb_ref[...]
    softplus = jnp.maximum(y, 0.0) + jnp.log1p(jnp.exp(-jnp.abs(y)))
    g = jnp.where(valid, -jnp.exp(alog_ref[...]) * softplus, 0.0)
    same_chunk_lower = ((row // chunk) == (col // chunk)) & (row >= col)
    gc_ref[...] = jnp.dot(jnp.where(same_chunk_lower, 1.0, 0.0), g, preferred_element_type=F32,
                          precision=HIGHEST)


def _gdn_gates(ba, a_log, dt_bias, *, t_valid, n_heads, tc):
    M = ba.shape[0]
    pad = lambda v: jnp.zeros((1, LANES), F32).at[0, n_heads:2 * n_heads].set(v.astype(F32))
    spec = pl.BlockSpec((tc, LANES), lambda i: (i, 0))
    vec = pl.BlockSpec((1, LANES), lambda i: (0, 0))
    return pl.pallas_call(
        functools.partial(_gdn_gates_kernel, chunk=GDN_CHUNK, t_valid=t_valid, n_heads=n_heads),
        grid=(M // tc,), in_specs=[spec, vec, vec], out_specs=[spec, spec],
        out_shape=[jax.ShapeDtypeStruct((M, LANES), F32)] * 2,
        compiler_params=_params(("arbitrary",), 0), name="gdn_gates")(ba, pad(a_log), pad(dt_bias))


def _gdn_kernel(q_ref, k_ref, v_ref, z_ref, cq_ref, ck_ref, cv_ref, wq_ref, wk_ref, wv_ref,
                bcol_ref, gcol_ref, grow_ref, s0_ref, onorm_ref, o_ref, sfin_ref,
                xq_ref, xk_ref, xv_ref, s_ref, *, hb, n_taps, dk, precise):
    c = pl.program_id(2)
    C = q_ref.shape[1]
    lead = xq_ref.shape[1] - C

    @pl.when(c == 0)
    def _():
        xq_ref[:, 0:lead, :] = cq_ref[:, 0]
        xk_ref[:, 0:lead, :] = ck_ref[:, 0]
        xv_ref[:, 0:lead, :] = cv_ref[:, 0]
        s_ref[...] = s0_ref[0]

    row = lax.broadcasted_iota(I32, (hb, C, C), 1)
    col = lax.broadcasted_iota(I32, (hb, C, C), 2)
    tri, stri = row >= col, row > col
    eye = jnp.where(row == col, 1.0, 0.0)
    def mxu(spec, a, b, three_pass):
        dot = lambda x, y: jnp.einsum(spec, x, y, preferred_element_type=F32)
        return _dot3(*_hi_lo(a), *_hi_lo(b), dot) if three_pass else dot(_bf(a), _bf(b))

    bmm = lambda a, b: mxu("hik,hkj->hij", a, b, precise)
    bmm3 = lambda a, b: mxu("hik,hkj->hij", a, b, True)

    def conv(x_ref, src_ref, w_ref):
        x_ref[:, lead:lead + C, :] = src_ref[...]
        first = lead - n_taps + 1
        y = x_ref[:, first:first + C, :] * w_ref[:, 0:1, :]
        for t in range(1, n_taps):
            y = y + x_ref[:, first + t:first + t + C, :] * w_ref[:, t:t + 1, :]
        x_ref[:, 0:lead, :] = x_ref[:, C:C + lead, :]
        return _silu(y)

    q = conv(xq_ref, q_ref, wq_ref)
    k = conv(xk_ref, k_ref, wk_ref)
    v = conv(xv_ref, v_ref, wv_ref)
    q = q * lax.rsqrt(jnp.sum(q * q, axis=-1, keepdims=True) + NORM_EPS) * (dk ** -0.5)
    k = k * lax.rsqrt(jnp.sum(k * k, axis=-1, keepdims=True) + NORM_EPS)
    bcol, gcol = bcol_ref[0], gcol_ref[0]
    beta = jnp.stack([bcol[:, j:j + 1] for j in range(hb)])
    gc = jnp.stack([gcol[:, j:j + 1] for j in range(hb)])
    gr = grow_ref[0, 0][:, None, :]
    decay = jnp.where(tri, jnp.exp(jnp.where(tri, gc - gr, 0.0)), 0.0)
    kb = k * beta
    eg = jnp.exp(gc)
    kq = mxu("hid,hjd->hij", jnp.concatenate([kb, q], axis=1), k, precise)
    a_mat = jnp.where(stri, kq[:, :C] * decay, 0.0)
    attn = jnp.where(tri, kq[:, C:] * decay, 0.0)
    inv = eye - a_mat
    pw = a_mat
    for _ in range(max(C.bit_length() - 2, 0)):
        pw = bmm3(pw, pw)
        inv = inv + bmm3(inv, pw)
    dv = v.shape[2]
    sol = bmm(inv, jnp.concatenate([v * beta, kb * eg], axis=2))
    u, w = sol[:, :, :dv], sol[:, :, dv:]
    s = s_ref[...]
    ws_qs = bmm(jnp.concatenate([w, q * eg], axis=1), s)
    v_new = u - ws_qs[:, :C]
    o = ws_qs[:, C:] + bmm(attn, v_new)
    g_last = gc[:, C - 1:C, :]
    k_dec = k * jnp.exp(g_last - gc)
    if precise:
        (k_hi, k_lo), (v_hi, v_lo) = _hi_lo(k_dec), _hi_lo(v_new)
        kv = [_dot3(k_hi[j], k_lo[j], v_hi[j], v_lo[j], _dot_tn) for j in range(hb)]
    else:
        k_bf, v_bf = _bf(k_dec), _bf(v_new)
        kv = [_dot_tn(k_bf[j], v_bf[j]) for j in range(hb)]
    s_new = s * jnp.exp(g_last) + jnp.stack(kv)
    s_ref[...] = s_new
    sfin_ref[0] = s_new
    o = o * lax.rsqrt(jnp.mean(o * o, axis=-1, keepdims=True) + NORM_EPS) * onorm_ref[...]
    o = (o * _silu(z_ref[...])).astype(o_ref.dtype)
    for j in range(hb):
        o_ref[:, j * LANES:(j + 1) * LANES] = o[j]


def _gdn_core(proj_hm, ba, conv_buf, s0, conv_w, a_log, dt_bias, o_norm, *, batch, n_chunks, t_valid, hb,
              precise):
    H = s0.shape[1]
    dk = s0.shape[2]
    C = GDN_CHUNK
    M = proj_hm.shape[1]
    n_taps = conv_w.shape[0]
    lead = 8
    nhb = H // hb
    beta, gc = _gdn_gates(ba, a_log, dt_bias, t_valid=t_valid, n_heads=H, tc=min(M, 256))
    beta_col = beta[:, :H].reshape(M, nhb, hb).transpose(1, 0, 2)
    gc_col = gc[:, H:2 * H].reshape(M, nhb, hb).transpose(1, 0, 2)
    gc_row = gc[:, H:2 * H].reshape(M // C, C, nhb, hb).transpose(2, 0, 3, 1)
    cb = conv_buf.astype(F32).reshape(batch, n_taps - 1, 3 * H, LANES).transpose(2, 0, 1, 3)
    cb = jnp.pad(cb, ((0, 0), (0, 0), (lead - (n_taps - 1), 0), (0, 0)))
    cw = jnp.pad(conv_w.astype(F32).reshape(n_taps, 3 * H, LANES).transpose(1, 0, 2),
                 ((0, 0), (0, 8 - n_taps), (0, 0)))

    def tile_spec(section):
        return pl.BlockSpec((hb, C, LANES), lambda b, h, c: (section * nhb + h, b * n_chunks + c, 0))

    def cb_spec(section):
        return pl.BlockSpec((hb, 1, lead, LANES), lambda b, h, c: (section * nhb + h, b, 0, 0))

    def cw_spec(section):
        return pl.BlockSpec((hb, 8, LANES), lambda b, h, c: (section * nhb + h, 0, 0))

    col_spec = pl.BlockSpec((1, C, hb), lambda b, h, c: (h, b * n_chunks + c, 0))
    in_specs = [tile_spec(0), tile_spec(1), tile_spec(2), tile_spec(3), cb_spec(0), cb_spec(1), cb_spec(2),
                cw_spec(0), cw_spec(1), cw_spec(2), col_spec, col_spec,
                pl.BlockSpec((1, 1, hb, C), lambda b, h, c: (h, b * n_chunks + c, 0, 0)),
                pl.BlockSpec((1, hb, dk, LANES), lambda b, h, c: (b, h, 0, 0)),
                pl.BlockSpec((1, LANES), lambda b, h, c: (0, 0))]
    out_specs = [pl.BlockSpec((C, hb * LANES), lambda b, h, c: (b * n_chunks + c, h)),
                 pl.BlockSpec((1, hb, dk, LANES), lambda b, h, c: (b, h, 0, 0))]
    out_shape = [jax.ShapeDtypeStruct((M, H * LANES), F32 if precise else BF16),
                 jax.ShapeDtypeStruct(s0.shape, F32)]
    scratch = [pltpu.VMEM((hb, C + lead, LANES), F32)] * 3 + [pltpu.VMEM((hb, dk, LANES), F32)]
    return pl.pallas_call(
        functools.partial(_gdn_kernel, hb=hb, n_taps=n_taps, dk=dk, precise=precise),
        grid=(batch, nhb, n_chunks), in_specs=in_specs, out_specs=out_specs, out_shape=out_shape,
        scratch_shapes=scratch, compiler_params=_params(("arbitrary",) * 3, 0), name="gdn_delta")(
            proj_hm, proj_hm, proj_hm, proj_hm, cb, cb, cb, cw, cw, cw, beta_col, gc_col, gc_row,
            s0.astype(F32), o_norm.reshape(1, LANES).astype(F32))


def _gdn_mixer(h, x, gate, conv_buf, s0, w_in, w_ba, conv_w, a_log, dt_bias, o_norm, w_out, *,
               batch, seq, rows_per_group, tm, precise):
    H = s0.shape[1]
    n_taps = conv_w.shape[0]
    C = GDN_CHUNK
    M = batch * seq
    qkvz = 4 * H * LANES
    proj_hm = _matmul(h, w_in, 0, ncols=qkvz, tm=tm, tn=512, head_major=True, precise=precise)
    ba = _matmul(h, w_ba, 0, tm=tm, tn=LANES, precise=precise)
    keep = min(seq, n_taps - 1)
    last_rows = proj_hm.reshape(4 * H, batch, seq, LANES)[:3 * H, :, seq - keep:]
    if seq % C == 0:
        n_chunks, t_valid, padded = seq // C, C, proj_hm
        ba_p = ba
    else:
        n_chunks, t_valid = 1, seq
        padded = jnp.pad(proj_hm.reshape(4 * H, batch, seq, LANES), ((0, 0), (0, 0), (0, C - seq), (0, 0)))
        padded = padded.reshape(4 * H, batch * C, LANES)
        ba_p = jnp.pad(ba.reshape(batch, seq, LANES), ((0, 0), (0, C - seq), (0, 0))).reshape(batch * C, LANES)
    o_rows, s_fin = _gdn_core(padded, ba_p, conv_buf, s0, conv_w, a_log, dt_bias, o_norm, batch=batch,
                            n_chunks=n_chunks, t_valid=t_valid, hb=min(16, H), precise=precise)
    o_rows = o_rows.reshape(batch, n_chunks * C, H * LANES)[:, :seq].reshape(M, H * LANES)
    x_new = _matmul(o_rows, w_out, 0, tm=tm, tn=512, resid=x, gate=gate, rows_per_group=rows_per_group,
                    precise=precise)
    new_buf = last_rows.transpose(1, 2, 0, 3).reshape(batch, keep, 3 * H * LANES)
    if keep < n_taps - 1:
        new_buf = jnp.concatenate([conv_buf.astype(F32)[:, keep:], new_buf], axis=1)
    return x_new, new_buf, s_fin


def _ordered_key(score):
    bits = pltpu.bitcast(score + 0.0, I32)
    return bits ^ ((bits >> 31) & 0x7FFFFFFF)


def _topk_select(keys, col, k):
    count = lambda m: jnp.sum(jnp.sum(m.astype(I32), axis=2, keepdims=True), axis=0, keepdims=True)
    n_rows = keys.shape[1]
    n_cols = keys.shape[0] * keys.shape[2]
    zero = jnp.zeros((1, n_rows, 1), I32)
    thr = jnp.where(count(keys >= zero) >= k, zero, jnp.full((1, n_rows, 1), INT32_MIN, I32))

    def bit_step(i, thr):
        cand = thr | (1 << (30 - i))
        return jnp.where(count(keys >= cand) >= k, cand, thr)

    thr = lax.fori_loop(0, 31, bit_step, thr)
    above = keys > thr
    tied = keys == thr
    need = k - count(above)
    n_bits = n_cols.bit_length()

    def tie_step(i, bound):
        cand = bound + (1 << (n_bits - 1 - i))
        ok = (cand <= n_cols) & (count(tied & (col < cand)) <= need)
        return jnp.where(ok, cand, bound)

    bound = lax.fori_loop(0, n_bits, tie_step, zero)
    return above | (tied & (col < bound))


def _dsa_prompt_kernel(qi_ref, wi_ref, ki_ref, q_ref, k_ref, v_ref, o_ref, sc_ref, bias_ref, ohm_ref, *,
                       hc, idx_scale, att_scale, topk, group, kt):
    i = pl.program_id(1)
    n_q, seq = sc_ref.shape
    n_chunks = qi_ref.shape[0] // hc
    n_live = (i * n_q + n_q - 1) // kt + 1

    def body(width):
        ki = ki_ref[0, 0:width, :]
        sc_ref[:, 0:width] = jnp.zeros((n_q, width), F32)

        def chunk(c, carry):
            lhs = qi_ref[pl.ds(c * hc, hc)].reshape(hc * n_q, qi_ref.shape[2])
            d = jnp.maximum(_dot_nt(lhs, ki), 0.0).reshape(hc, n_q, width)
            w = wi_ref[c]
            acc = d[0] * w[:, 0:1]
            for j in range(1, hc):
                acc = acc + d[j] * w[:, j:j + 1]
            sc_ref[:, 0:width] += acc
            return carry

        lax.fori_loop(0, n_chunks, chunk, 0)
        col = lax.broadcasted_iota(I32, (n_q, width), 1)
        allowed = col <= i * n_q + lax.broadcasted_iota(I32, (n_q, width), 0)
        keys = jnp.where(allowed, _ordered_key(sc_ref[:, 0:width] * idx_scale), INT32_MIN)
        sel = _topk_select(keys[None], col[None], topk)[0] & allowed
        bias_ref[:, 0:width] = jnp.where(sel, 0.0, -jnp.inf)

        def grp(g, carry):
            qg = q_ref[pl.ds(g * group, group)].reshape(group * n_q, q_ref.shape[2])
            s = _dot_nt(qg, k_ref[g, 0:width, :]) * att_scale
            s = s.reshape(group, n_q, width) + bias_ref[:, 0:width][None]
            p = jnp.exp(s - jnp.max(s, axis=-1, keepdims=True))
            l = jnp.sum(p, axis=-1, keepdims=True)
            o = jnp.dot(p.reshape(group * n_q, width).astype(BF16), v_ref[g, 0:width, :],
                        preferred_element_type=F32)
            ohm_ref[pl.ds(g * group, group)] = (o.reshape(group, n_q, o.shape[-1]) / l).astype(ohm_ref.dtype)
            return carry

        lax.fori_loop(0, k_ref.shape[0], grp, 0)

    for t in range(seq // kt):
        pl.when(n_live == t + 1)(functools.partial(body, (t + 1) * kt))
    dh = ohm_ref.shape[2]
    for h in range(ohm_ref.shape[0]):
        o_ref[:, h * dh:(h + 1) * dh] = ohm_ref[h]


def _dsa_prompt(q_hm, qi_hm, wi, ki, kv_hm, *, batch, seq, topk):
    HQ, M, dh = q_hm.shape
    HI = qi_hm.shape[0]
    G = kv_hm.shape[0] // 2
    nq = ATT_BLOCK
    nb = seq // nq
    hc = 4
    kt = 512 if seq % 512 == 0 else seq
    wi_c = wi.reshape(M, HI // hc, hc).transpose(1, 0, 2)
    di = qi_hm.shape[2]
    group = HQ // G
    kern = functools.partial(_dsa_prompt_kernel, hc=hc, idx_scale=float(di ** -0.5 * HI ** -0.5),
                             att_scale=float(dh ** -0.5), topk=topk, group=group, kt=kt)
    vmem = (2 * HI * nq * di * 2 + 5 * HQ * nq * dh * 2 + 4 * G * seq * dh * 2 + 2 * seq * di * 2
            + 8 * nq * seq * 4 + 3 * hc * nq * seq * 4 + 4 * group * nq * seq * 4 + (4 << 20))
    return pl.pallas_call(
        kern, grid=(batch, nb),
        in_specs=[pl.BlockSpec((HI, nq, di), lambda b, i: (0, b * nb + i, 0)),
                  pl.BlockSpec((HI // hc, nq, hc), lambda b, i: (0, b * nb + i, 0)),
                  pl.BlockSpec((1, seq, di), lambda b, i: (b, 0, 0)),
                  pl.BlockSpec((HQ, nq, dh), lambda b, i: (0, b * nb + i, 0)),
                  pl.BlockSpec((G, seq, dh), lambda b, i: (0, b, 0)),
                  pl.BlockSpec((G, seq, dh), lambda b, i: (1, b, 0))],
        out_specs=pl.BlockSpec((nq, HQ * dh), lambda b, i: (b * nb + i, 0)),
        out_shape=jax.ShapeDtypeStruct((M, HQ * dh), BF16),
        scratch_shapes=[pltpu.VMEM((nq, seq), F32), pltpu.VMEM((nq, seq), F32), pltpu.VMEM((HQ, nq, dh), BF16)],
        compiler_params=_params(("arbitrary", "arbitrary"), vmem), name="dsa_prompt")(
            qi_hm, wi_c, ki, q_hm, kv_hm, kv_hm)


def _dsa_sample_scores_kernel(pt_ref, qi_ref, wi_ref, pool_ref, new_ref, o_ref, *, n_pages, n_tok, idx_scale):
    p = pl.program_id(1)
    page = jnp.where(p < n_pages, pool_ref[0], new_ref[0])
    d = jnp.maximum(_dot3(*_hi_lo(qi_ref[0]), *_hi_lo(page), _dot_nt), 0.0) * wi_ref[0]
    hi = d.shape[0] // n_tok
    o_ref[0] = jnp.sum(d.reshape(n_tok, hi, d.shape[1]), axis=1) * idx_scale


def _dsa_sample_select_kernel(sc_ref, o_ref, *, n_past, topk):
    sc = sc_ref[0]
    col = lax.broadcasted_iota(I32, sc.shape, 1)
    q_pos = n_past + lax.broadcasted_iota(I32, sc.shape, 0)
    allowed = col <= q_pos
    keys = jnp.where(allowed, _ordered_key(sc), INT32_MIN)
    sel = _topk_select(keys[None], col[None], topk)[0] & allowed
    o_ref[0] = jnp.where(sel, 0.0, -jnp.inf)


def _dsa_sample_attn_kernel(pt_ref, q_ref, bias_ref, hmask_ref, kp_ref, vp_ref, kn_ref, vn_ref, o_ref,
                            m_ref, l_ref, acc_ref, *, n_pages, att_scale):
    p = pl.program_id(1)
    n_tok = bias_ref.shape[1]
    heads = q_ref.shape[1] // n_tok

    @pl.when(p == 0)
    def _():
        m_ref[...] = jnp.full_like(m_ref, -jnp.inf)
        l_ref[...] = jnp.zeros_like(l_ref)
        acc_ref[...] = jnp.zeros_like(acc_ref)

    def step(k_ref, v_ref):
        cols = k_ref.shape[1] * k_ref.shape[2]
        k_hi, k_lo = _hi_lo(k_ref[0].reshape(cols, k_ref.shape[3]))
        v_hi, v_lo = _hi_lo(v_ref[0].reshape(cols, v_ref.shape[3]))
        bias = jnp.concatenate([jnp.broadcast_to(bias_ref[0, t:t + 1, :], (heads, cols)) for t in range(n_tok)],
                               axis=0)
        s = _dot3(*_hi_lo(q_ref[0]), k_hi, k_lo, _dot_nt) * att_scale + (bias + hmask_ref[...])
        m_old = m_ref[...]
        m_new = jnp.maximum(m_old, jnp.max(s, axis=-1, keepdims=True))
        m_safe = jnp.where(m_new == -jnp.inf, 0.0, m_new)
        alpha = jnp.exp(m_old - m_safe)
        pr = jnp.exp(s - m_safe)
        l_ref[...] = alpha * l_ref[...] + jnp.sum(pr, axis=-1, keepdims=True)
        pv = _dot3(*_hi_lo(pr), v_hi, v_lo, functools.partial(jnp.dot, preferred_element_type=F32))
        acc_ref[...] = alpha * acc_ref[...] + pv
        m_ref[...] = m_new

    pl.when(p < n_pages)(functools.partial(step, kp_ref, vp_ref))
    pl.when(p == n_pages)(functools.partial(step, kn_ref, vn_ref))

    @pl.when(p == pl.num_programs(1) - 1)
    def _():
        o_ref[0] = (acc_ref[...] / l_ref[...]).astype(o_ref.dtype)


def _dsa_sample(q, qi, wi, ki_new, k_new, v_new, k_pool, v_pool, ki_pool, page_table, *, topk):
    B, T, _ = q.shape
    n_pool, page, G, dh = k_pool.shape
    di = ki_pool.shape[2]
    HI = wi.shape[2]
    HQ = q.shape[2] // dh
    n_rep = HQ // G
    P = page_table.shape[1]
    n_past = P * page
    s_pad = (P + 1) * page
    idx_scale = float(di ** -0.5 * HI ** -0.5)
    pad_rows = lambda a: jnp.pad(a, ((0, 0), (0, page - T)) + ((0, 0),) * (a.ndim - 2))
    last = lambda b, p, pt: pt[b, jnp.minimum(p, P - 1)]

    scores = pl.pallas_call(
        functools.partial(_dsa_sample_scores_kernel, n_pages=P, n_tok=T, idx_scale=idx_scale),
        grid_spec=pltpu.PrefetchScalarGridSpec(
            num_scalar_prefetch=1, grid=(B, P + 1),
            in_specs=[pl.BlockSpec((1, T * HI, di), lambda b, p, pt: (b, 0, 0)),
                      pl.BlockSpec((1, T * HI, 1), lambda b, p, pt: (b, 0, 0)),
                      pl.BlockSpec((1, page, di), lambda b, p, pt: (last(b, p, pt), 0, 0)),
                      pl.BlockSpec((1, page, di), lambda b, p, pt: (b, 0, 0))],
            out_specs=pl.BlockSpec((1, T, page), lambda b, p, pt: (b, 0, p))),
        out_shape=jax.ShapeDtypeStruct((B, T, s_pad), F32),
        compiler_params=_params(("arbitrary", "arbitrary"), 0), name="dsa_sample_scores")(
            page_table, qi.reshape(B, T * HI, di), wi.reshape(B, T * HI, 1), ki_pool, pad_rows(ki_new))

    bias = pl.pallas_call(
        functools.partial(_dsa_sample_select_kernel, n_past=n_past, topk=topk),
        grid=(B,), in_specs=[pl.BlockSpec((1, T, s_pad), lambda b: (b, 0, 0))],
        out_specs=pl.BlockSpec((1, T, s_pad), lambda b: (b, 0, 0)),
        out_shape=jax.ShapeDtypeStruct((B, T, s_pad), F32),
        compiler_params=_params(("arbitrary",), 0), name="dsa_sample_select")(scores)

    rows, cols = T * HQ, page * G
    row_g = (jnp.arange(rows, dtype=I32) % HQ) // n_rep
    hmask = jnp.where(row_g[:, None] == jnp.arange(cols, dtype=I32)[None, :] % G, 0.0, -jnp.inf).astype(F32)
    o = pl.pallas_call(
        functools.partial(_dsa_sample_attn_kernel, n_pages=P, att_scale=float(dh ** -0.5)),
        grid_spec=pltpu.PrefetchScalarGridSpec(
            num_scalar_prefetch=1, grid=(B, P + 1),
            in_specs=[pl.BlockSpec((1, rows, dh), lambda b, p, pt: (b, 0, 0)),
                      pl.BlockSpec((1, T, cols), lambda b, p, pt: (b, 0, p)),
                      pl.BlockSpec((rows, cols), lambda b, p, pt: (0, 0)),
                      pl.BlockSpec((1, page, G, dh), lambda b, p, pt: (last(b, p, pt), 0, 0, 0)),
                      pl.BlockSpec((1, page, G, dh), lambda b, p, pt: (last(b, p, pt), 0, 0, 0)),
                      pl.BlockSpec((1, page, G, dh), lambda b, p, pt: (b, 0, 0, 0)),
                      pl.BlockSpec((1, page, G, dh), lambda b, p, pt: (b, 0, 0, 0))],
            out_specs=pl.BlockSpec((1, rows, dh), lambda b, p, pt: (b, 0, 0)),
            scratch_shapes=[pltpu.VMEM((rows, 1), F32), pltpu.VMEM((rows, 1), F32),
                            pltpu.VMEM((rows, dh), F32)]),
        out_shape=jax.ShapeDtypeStruct((B, rows, dh), F32),
        compiler_params=_params(("arbitrary", "arbitrary"), 0), name="dsa_sample_attn")(
            page_table, q.reshape(B, rows, dh), jnp.repeat(bias, G, axis=2), hmask, k_pool, v_pool,
            pad_rows(k_new), pad_rows(v_new))
    return o.reshape(B, T, HQ * dh)


def _gather_rows_kernel(idx_ref, rows_ref, x_hbm, o_ref, buf_ref, sem, *, tm):
    i = pl.program_id(0)
    base = i * tm
    n_rows = rows_ref[i]

    def copy(r):
        return pltpu.make_async_copy(x_hbm.at[pl.ds(idx_ref[base + r], 1)], buf_ref.at[pl.ds(r, 1)], sem)

    def start(r, c):
        copy(r).start()
        return c

    def wait(r, c):
        copy(r).wait()
        return c

    buf_ref[...] = jnp.zeros_like(buf_ref)
    lax.fori_loop(0, n_rows, start, 0)
    lax.fori_loop(0, n_rows, wait, 0)
    o_ref[...] = buf_ref[...].astype(o_ref.dtype)


def _gather_rows(x, idx, rows, *, tm, out_dtype):
    S = idx.shape[0]
    W = x.shape[1]
    return pl.pallas_call(
        functools.partial(_gather_rows_kernel, tm=tm),
        grid_spec=pltpu.PrefetchScalarGridSpec(
            num_scalar_prefetch=2, grid=(S // tm,),
            in_specs=[pl.BlockSpec(memory_space=pl.ANY)],
            out_specs=pl.BlockSpec((tm, W), lambda i, idx, rows: (i, 0)),
            scratch_shapes=[pltpu.VMEM((tm, W), x.dtype), pltpu.SemaphoreType.DMA]),
        out_shape=jax.ShapeDtypeStruct((S, W), out_dtype),
        compiler_params=_params(("arbitrary",), 6 * tm * W * 4), name="moe_gather")(idx, rows, x)


def _combine_kernel(slot_ref, y_hbm, x_ref, gt_ref, rg_ref, fg_ref, o_ref, buf_ref, sem, *, tm, n_top, row0):
    base = row0 + pl.program_id(0) * tm

    def copy(r, j):
        return pltpu.make_async_copy(y_hbm.at[pl.ds(slot_ref[(base + r) * n_top + j], 1)],
                                     buf_ref.at[j, pl.ds(r, 1)], sem)

    def start(r, c):
        for j in range(n_top):
            copy(r, j).start()
        return c

    def wait(r, c):
        for j in range(n_top):
            copy(r, j).wait()
        return c

    lax.fori_loop(0, tm, start, 0)
    lax.fori_loop(0, tm, wait, 0)
    rg = rg_ref[...]
    moe = buf_ref[0] * rg[:, 0:1]
    for j in range(1, n_top):
        moe = moe + buf_ref[j] * rg[:, j:j + 1]
    x = x_ref[...] + gt_ref[0] * moe
    o_ref[...] = x * lax.rsqrt(jnp.mean(x * x, axis=-1, keepdims=True) + NORM_EPS) * fg_ref[...]


def _combine(y_slots, slot_of, x, gate, route_gate, final_g, *, rows_per_group, tm, row0):
    M, D = x.shape
    tiles_per_group = rows_per_group // tm
    t0 = row0 // tm
    return pl.pallas_call(
        functools.partial(_combine_kernel, tm=tm, n_top=MOE_TOPK, row0=row0),
        grid_spec=pltpu.PrefetchScalarGridSpec(
            num_scalar_prefetch=1, grid=(M // tm,),
            in_specs=[pl.BlockSpec(memory_space=pl.ANY),
                      pl.BlockSpec((tm, D), lambda i, s: (i, 0)),
                      pl.BlockSpec((1, gate.shape[1], D), lambda i, s: (i // tiles_per_group, 0, 0)),
                      pl.BlockSpec((tm, LANES), lambda i, s: (i + t0, 0)),
                      pl.BlockSpec((1, D), lambda i, s: (0, 0))],
            out_specs=pl.BlockSpec((tm, D), lambda i, s: (i, 0)),
            scratch_shapes=[pltpu.VMEM((MOE_TOPK, tm, D), F32), pltpu.SemaphoreType.DMA]),
        out_shape=jax.ShapeDtypeStruct((M, D), F32),
        compiler_params=_params(("arbitrary",), (MOE_TOPK + 8) * tm * D * 4), name="moe_combine")(
            slot_of, y_slots, x, gate, route_gate, final_g.reshape(1, D))


def _tile_tables(rows, expert):
    t = jnp.arange(rows.shape[0], dtype=I32)
    seen = (t[None, :] <= t[:, None]) & (rows[None, :] > 0)
    tile = jnp.max(jnp.where(seen, t[None, :], 0), axis=1)
    expert = expert[tile]
    fresh = jnp.concatenate([jnp.ones((1,), I32), (expert[1:] != expert[:-1]).astype(I32)])
    return expert.astype(I32), fresh, rows.astype(I32), tile.astype(I32)


def _route_tables(expert_idx, n_experts, align, tm):
    N, top = expert_idx.shape
    flat = expert_idx.reshape(-1)
    ids = jnp.arange(n_experts, dtype=I32)
    onehot = (flat[:, None] == ids[None, :]).astype(I32)
    rank = jnp.sum((jnp.cumsum(onehot, axis=0) - 1) * onehot, axis=1)
    counts = jnp.sum(onehot, axis=0)
    blocks = (counts + align - 1) // align
    blk_end = jnp.cumsum(blocks)
    blk_start = blk_end - blocks
    n_blocks = (N * top + n_experts * (align - 1)) // align
    slot_of = (jnp.sum(onehot * blk_start[None, :], axis=1) * align + rank).astype(I32)
    src_row = jnp.zeros((n_blocks * align,), I32).at[slot_of].set(jnp.arange(N * top, dtype=I32) // top)
    b = jnp.arange(n_blocks, dtype=I32)
    blk_expert = jnp.minimum(jnp.sum((blk_end[None, :] <= b[:, None]).astype(I32), axis=1), n_experts - 1)
    blk_rows = jnp.clip(counts[blk_expert] - (b - blk_start[blk_expert]) * align, 0, align)
    blk_rows = jnp.where(b < blk_end[-1], blk_rows, 0)
    per = align // tm
    tile_rows = jnp.clip(jnp.repeat(blk_rows, per) - jnp.tile(jnp.arange(per, dtype=I32) * tm, n_blocks), 0, tm)
    t = jnp.arange(n_blocks * per, dtype=I32)
    in_group = jnp.repeat(b, per) < blk_end[-1]
    last_expert = blk_expert[jnp.maximum(blk_end[-1] - 1, 0)]
    tile_expert = jnp.where(in_group, jnp.repeat(blk_expert, per), last_expert)
    phys = jnp.where(in_group, (blk_start[tile_expert] + blk_end[tile_expert]) * per - 1 - t, t)
    v_rows = tile_rows[phys]
    live = v_rows > 0
    first_live = jnp.min(jnp.where(live, t, t.shape[0] - 1))
    seen = (t[None, :] <= t[:, None]) & live[None, :]
    src_step = jnp.maximum(jnp.max(jnp.where(seen, t[None, :], 0), axis=1), first_live)
    fresh = jnp.concatenate([jnp.ones((1,), I32), (tile_expert[1:] != tile_expert[:-1]).astype(I32)])
    visit = (tile_expert.astype(I32), fresh, v_rows.astype(I32), phys[src_step].astype(I32), phys.astype(I32))
    return src_row, slot_of, tile_rows.astype(I32), visit, _tile_tables(blk_rows, blk_expert)


MOE_ALIGN = 1024
MOE_TILE = 512
MOE_SUB = 128


def _moe(h_rows, expert_idx, w1, w3, w2):
    src_row, slot_of, tile_rows, visit, blk_tables = _route_tables(expert_idx, w1.shape[0], MOE_ALIGN, MOE_TILE)
    xg = _gather_rows(h_rows, src_row, tile_rows, tm=MOE_TILE, out_dtype=BF16)
    hmid = _swiglu_in(xg, w1, w3, None, tm=MOE_TILE, tn=512, tile_tables=visit, sub=MOE_SUB)
    y = _expert_out(hmid, w2, blk_tables, tm=MOE_ALIGN, tk=256, sub=MOE_SUB)
    return y, slot_of


def _pad_cols(w, width):
    return jnp.pad(w, ((0, 0), (0, 0), (0, width - w.shape[2])))


def kernel(x_prompt, x_sample, state_gdn, state_conv, cache_k, cache_v, cache_kidx, page_table, c_prompt, c_sample, ada_w, ada_b, norm1_g, norm2_g, gdn_w_in, gdn_conv_w, gdn_a_log, gdn_dt_bias, gdn_o_norm, gdn_w_out, att_w_in, att_w_out, ffn_w1, ffn_w3, ffn_w2, moe_router_w, moe_w1, moe_w3, moe_w2, final_norm_g):
    BP, TP, D = x_prompt.shape
    BS, TS, _ = x_sample.shape
    MP, MS = BP * TP, BS * TS
    H = state_gdn.shape[2]
    G, dh = cache_k.shape[3], cache_k.shape[4]
    di = cache_kidx.shape[3]
    HQ = att_w_out.shape[1] // dh
    q_w, kv_w = HQ * dh, G * dh
    HI = (att_w_in.shape[2] - q_w - 2 * kv_w - di) // (di + 1)
    qi_w = HI * di
    n_experts = moe_w1.shape[1]
    tmp = 512 if MP % 512 == 0 else MP
    tms = MS

    n_c = BP + BS
    c_all = jnp.pad(jnp.concatenate([c_prompt, c_sample], axis=0), ((0, -n_c % 16), (0, 0)))
    mods = [_matmul(c_all, ada_w, l, tm=c_all.shape[0], tn=512, a_silu=True, bias=ada_b.reshape(-1, 1, 6 * D),
                    precise=True) for l in range(ada_w.shape[0])]

    def mod_p(l, j):
        return mods[l][:BP, j * D:(j + 1) * D].reshape(BP, 1, D)

    def mod_s(l, j):
        return jnp.repeat(mods[l][BP:n_c, j * D:(j + 1) * D], TS, axis=0).reshape(1, MS, D)

    xp = x_prompt.reshape(MP, D)
    xs = x_sample.reshape(MS, D)

    hp = _modulate(xp, norm1_g[0], mod_p(0, 0), mod_p(0, 1), rows_per_group=TP, tm=256, out_dtype=BF16)
    hs = _modulate(xs, norm1_g[0], mod_s(0, 0), mod_s(0, 1), rows_per_group=MS, tm=tms, out_dtype=F32)
    qkvz = 4 * H * LANES
    w_ba = _pad_cols(gdn_w_in[:, :, qkvz:], LANES)
    zeros_buf = jnp.zeros((BP,) + state_conv.shape[2:], F32)
    zeros_s = jnp.zeros((BP,) + state_gdn.shape[2:], F32)
    gdn = functools.partial(_gdn_mixer, w_in=gdn_w_in, w_ba=w_ba, conv_w=gdn_conv_w[0], a_log=gdn_a_log[0],
                            dt_bias=gdn_dt_bias[0], o_norm=gdn_o_norm[0], w_out=gdn_w_out)
    xp, conv_p, s_p = gdn(hp, xp, mod_p(0, 2), zeros_buf, zeros_s, batch=BP, seq=TP, rows_per_group=TP, tm=tmp,
                          precise=False)
    xs, conv_s, s_s = gdn(hs, xs, mod_s(0, 2), state_conv[0], state_gdn[0], batch=BS, seq=TS,
                          rows_per_group=MS, tm=tms, precise=True)

    def ffn(x, shift, scale, gate, rows_per_group, tm, tmod, precise):
        tn = 256 if precise else 512
        h = _modulate(x, norm2_g[0], shift, scale, rows_per_group=rows_per_group, tm=tmod,
                      out_dtype=F32 if precise else BF16)
        mid = _swiglu_in(h, ffn_w1, ffn_w3, 0, tm=tm, tn=tn, precise=precise)
        k_parts = 2 if mid.shape[1] % (2 * LANES) == 0 else 1
        for part in range(k_parts):
            x = _matmul(mid, ffn_w2, 0, tm=tm, tn=tn, resid=x, gate=gate, rows_per_group=rows_per_group,
                        k_parts=k_parts, k_part=part, precise=precise)
        return x

    xp = ffn(xp, mod_p(0, 3), mod_p(0, 4), mod_p(0, 5), TP, tmp, 256, False)
    xs = ffn(xs, mod_s(0, 3), mod_s(0, 4), mod_s(0, 5), MS, tms, tms, True)

    hp = _modulate(xp, norm1_g[1], mod_p(1, 0), mod_p(1, 1), rows_per_group=TP, tm=256, out_dtype=BF16)
    hs = _modulate(xs, norm1_g[1], mod_s(1, 0), mod_s(1, 1), rows_per_group=MS, tm=tms, out_dtype=F32)
    w_kiwi = _pad_cols(att_w_in[:, :, q_w + 2 * kv_w + qi_w:], 2 * LANES)

    def att_proj(h, tm, head_major, precise):
        q = _matmul(h, att_w_in, 0, col0=0, ncols=q_w, tm=tm, tn=512, out_dtype=F32 if precise else BF16,
                    head_major=head_major, precise=precise)
        kv = _matmul(h, att_w_in, 0, col0=q_w, ncols=2 * kv_w, tm=tm, tn=512,
                     head_major="both" if head_major else False, precise=precise)
        qi = _matmul(h, att_w_in, 0, col0=q_w + 2 * kv_w, ncols=qi_w, tm=tm, tn=512,
                     out_dtype=F32 if precise else BF16, head_major=head_major, precise=precise)
        kiwi = _matmul(h, w_kiwi, 0, tm=tm, tn=2 * LANES, precise=precise)
        return q, kv, qi, kiwi[:, :di], kiwi[:, di:di + HI]

    q, (kv_p, kv_hm), qi, ki_p, wi = att_proj(hp, tmp, True, False)
    k_p, v_p = kv_p[:, :kv_w], kv_p[:, kv_w:]
    o_rows = _dsa_prompt(q, qi, wi, ki_p.astype(BF16).reshape(BP, TP, di), kv_hm, batch=BP, seq=TP,
                         topk=min(IDX_TOPK, TP // 4))
    xp = _matmul(o_rows, att_w_out, 0, tm=tmp, tn=512, resid=xp, gate=mod_p(1, 2), rows_per_group=TP)

    q, kv_s, qi, ki_s, wi = att_proj(hs, tms, False, True)
    k_s, v_s = kv_s[:, :kv_w], kv_s[:, kv_w:]
    n_past = page_table.shape[1] * cache_k.shape[2]
    o_s = _dsa_sample(q.reshape(BS, TS, q_w), qi.reshape(BS, TS, qi_w), wi.reshape(BS, TS, HI),
                      ki_s.reshape(BS, TS, di), k_s.reshape(BS, TS, G, dh), v_s.reshape(BS, TS, G, dh),
                      cache_k.reshape(cache_k.shape[1:]), cache_v.reshape(cache_v.shape[1:]),
                      cache_kidx.reshape(cache_kidx.shape[1:]), page_table,
                      topk=min(IDX_TOPK, (n_past + TS) // 4))
    xs = _matmul(o_s.reshape(MS, q_w), att_w_out, 0, tm=tms, tn=512, resid=xs, gate=mod_s(1, 2),
                 rows_per_group=MS, precise=True)

    rw = jnp.pad(moe_router_w[0], ((0, 0), (0, LANES - n_experts)))
    hp, ridx_p, rgate_p = _modulate(xp, norm2_g[1], mod_p(1, 3), mod_p(1, 4), rows_per_group=TP, tm=256,
                                    out_dtype=F32, router_w=rw, n_experts=n_experts)
    hs, ridx_s, rgate_s = _modulate(xs, norm2_g[1], mod_s(1, 3), mod_s(1, 4), rows_per_group=MS, tm=tms,
                                    out_dtype=F32, router_w=rw, n_experts=n_experts)
    h_all = jnp.concatenate([hp, hs], axis=0)
    ridx = jnp.concatenate([ridx_p, ridx_s], axis=0)[:, :MOE_TOPK]
    rgate = jnp.concatenate([rgate_p, rgate_s], axis=0)
    drop_layer = lambda w: w.reshape(w.shape[1:])
    y_slots, slot_of = _moe(h_all, ridx, drop_layer(moe_w1), drop_layer(moe_w3), drop_layer(moe_w2))
    y_prompt = _combine(y_slots, slot_of, xp, mod_p(1, 5), rgate, final_norm_g, rows_per_group=TP, tm=256, row0=0)
    y_sample = _combine(y_slots, slot_of, xs, mod_s(1, 5), rgate, final_norm_g, rows_per_group=MS, tm=tms,
                        row0=MP)

    f32 = lambda a: a.astype(F32)
    return (y_prompt.reshape(BP, TP, D), y_sample.reshape(BS, TS, D),
            s_p[None], conv_p[None], s_s[None], conv_s[None],
            f32(k_p).reshape(1, BP, TP, G, dh), f32(v_p).reshape(1, BP, TP, G, dh), f32(ki_p).reshape(1, BP, TP, di),
            f32(k_s).reshape(1, BS, TS, G, dh), f32(v_s).reshape(1, BS, TS, G, dh), f32(ki_s).reshape(1, BS, TS, di))
```
